```python
import math
import jax, jax.numpy as jnp
from jax import lax
import numpy as np

D_MODEL = 1024
BATCH = 8
SEQ = 4096
DEPTH = 1

HG_HEADS = 4
HG_DK = 128
HG_DV = 128
HG_KEY = HG_HEADS * HG_DK
HG_WIDTH = HG_HEADS * HG_DV
HG_CHUNK = 64
DA_HEADS = 4
DA_HD = 64
DA_DV = 2 * DA_HD
DA_QK = DA_HEADS * 2 * DA_HD
DA_WIDTH = DA_HEADS * DA_DV
Q_BLOCK = 128
N_EXPERTS = 32
TOP_K = 4
D_FF = 1024
SWIGLU_LIMIT = 7.0
SWIGLU_ALPHA = 1.702
DN_ALPHA = (2.0 * DEPTH) ** 0.25
DN_BETA = (8.0 * DEPTH) ** -0.25
LN_EPS = 1e-5
NORM_EPS = 1e-6
IN_SIZES = (HG_KEY, HG_KEY, HG_WIDTH, HG_WIDTH, DA_QK, DA_QK, DA_WIDTH, D_MODEL, D_MODEL)
D_IN = sum(IN_SIZES)

kernel_name = "hybrid_hgrn2_diffattn_moe_deepnorm"


def layer_norm(x, g, b):
    xf = x.astype(jnp.float32)
    mu = jnp.mean(xf, axis=-1, keepdims=True)
    xc = xf - mu
    var = jnp.mean(xc * xc, axis=-1, keepdims=True)
    y = xc * lax.rsqrt(var + LN_EPS) * g.astype(jnp.float32) + b.astype(jnp.float32)
    return y.astype(x.dtype)


def head_rms_norm(o, g):
    ms = jnp.mean(o * o, axis=-1, keepdims=True)
    return o * lax.rsqrt(ms + NORM_EPS) * g.astype(jnp.float32)


def alibi_slopes(n):
    return jnp.asarray([2.0 ** (-8.0 * (h + 1) / n) for h in range(n)], dtype=jnp.float32)


def hgrn2_branch(q, f_logit, i, g, lb, norm_g):
    B, S, _ = q.shape
    nc = S // HG_CHUNK
    f32 = jnp.float32
    f = lb + (1.0 - lb) * jax.nn.sigmoid(f_logit.astype(f32))
    log_f = jnp.log(f)
    k = 1.0 - f
    qa = jax.nn.silu(q.astype(f32))
    v = i.astype(f32)

    def to_chunks(t, d):
        return t.reshape(B, nc, HG_CHUNK, HG_HEADS, d).transpose(1, 0, 3, 2, 4)

    qc, kc, lc = to_chunks(qa, HG_DK), to_chunks(k, HG_DK), to_chunks(log_f, HG_DK)
    vc = to_chunks(v, HG_DV)
    causal = jnp.tril(jnp.ones((HG_CHUNK, HG_CHUNK), dtype=bool))[None, None, :, :, None]

    def step(state, inp):
        q_, k_, v_, l_ = inp
        bcum = jnp.cumsum(l_, axis=2)
        o_inter = jnp.einsum('bhtk,bhkv->bhtv', q_ * jnp.exp(bcum), state)
        diff = bcum[:, :, :, None, :] - bcum[:, :, None, :, :]
        decay = jnp.exp(jnp.where(causal, diff, -jnp.inf))
        attn = jnp.einsum('bhtk,bhsk,bhtsk->bhts', q_, k_, decay)
        o_intra = jnp.einsum('bhts,bhsv->bhtv', attn, v_)
        b_last = bcum[:, :, -1:, :]
        k_dec = k_ * jnp.exp(b_last - bcum)
        new_state = jnp.exp(b_last[:, :, 0, :])[..., None] * state + jnp.einsum('bhsk,bhsv->bhkv', k_dec, v_)
        return new_state, o_inter + o_intra

    state0 = jnp.zeros((B, HG_HEADS, HG_DK, HG_DV), f32)
    _, o = lax.scan(step, state0, (qc, kc, vc, lc))
    o = o.transpose(1, 0, 3, 2, 4).reshape(B, S, HG_HEADS, HG_DV)
    o = head_rms_norm(o, norm_g).reshape(B, S, HG_WIDTH)
    o = o * jax.nn.sigmoid(g.astype(f32))
    return o.astype(q.dtype)


def diff_attention_branch(q, k, v, lam_params, lambda_init, norm_g):
    B, S, _ = q.shape
    nb = S // Q_BLOCK
    f32 = jnp.float32
    qh = q.reshape(B, S, DA_HEADS, 2, DA_HD).transpose(0, 2, 3, 1, 4) * (DA_HD ** -0.5)
    kh = k.reshape(B, S, DA_HEADS, 2, DA_HD).transpose(0, 2, 3, 1, 4)
    vh = v.reshape(B, S, DA_HEADS, DA_DV).transpose(0, 2, 1, 3)
    lp = lam_params.astype(f32)
    lam = jnp.exp(jnp.sum(lp[0] * lp[1])) - jnp.exp(jnp.sum(lp[2] * lp[3])) + lambda_init
    slopes = alibi_slopes(DA_HEADS)
    kpos = jnp.arange(S)
    qb = qh.reshape(B, DA_HEADS, 2, nb, Q_BLOCK, DA_HD).transpose(3, 0, 1, 2, 4, 5)

    def block(args):
        qblk, bi = args
        qpos = bi * Q_BLOCK + jnp.arange(Q_BLOCK)
        s = jnp.einsum('bhjqd,bhjkd->bhjqk', qblk, kh).astype(f32)
        dist = qpos[:, None] - kpos[None, :]
        bias = -slopes[:, None, None] * dist.astype(f32)
        s = jnp.where(dist >= 0, s + bias[None, :, None], -jnp.inf)
        p = jax.nn.softmax(s, axis=-1)
        a = p[:, :, 0] - lam * p[:, :, 1]
        return jnp.einsum('bhqk,bhkv->bhqv', a.astype(vh.dtype), vh)

    o = lax.map(block, (qb, jnp.arange(nb)))
    o = o.transpose(1, 0, 3, 2, 4).reshape(B, S, DA_HEADS, DA_DV).astype(f32)
    o = head_rms_norm(o, norm_g) * (1.0 - lambda_init)
    return o.reshape(B, S, DA_WIDTH).astype(q.dtype)


def moe_ffn(x, router_w, router_b, w_gate_up, b_gate_up, w_down, b_down):
    B, S, D = x.shape
    xt = x.reshape(B * S, D)
    logits = (xt @ router_w + router_b).astype(jnp.float32)
    top_vals, top_idx = lax.top_k(logits, TOP_K)
    top_w = jax.nn.softmax(top_vals, axis=-1)
    combine = jnp.sum(jax.nn.one_hot(top_idx, N_EXPERTS, dtype=jnp.float32) * top_w[..., None], axis=1)

    def expert_step(acc, params):
        wgu, bgu, wd, bd, cw = params
        h = xt @ wgu + bgu
        gate = jnp.minimum(h[:, 0::2], SWIGLU_LIMIT)
        up = jnp.clip(h[:, 1::2], -SWIGLU_LIMIT, SWIGLU_LIMIT)
        glu = gate * jax.nn.sigmoid(gate * SWIGLU_ALPHA)
        y = ((up + 1.0) * glu) @ wd + bd
        return acc + cw[:, None] * y.astype(jnp.float32), None

    acc0 = jnp.zeros((B * S, D), jnp.float32)
    acc, _ = lax.scan(expert_step, acc0, (w_gate_up, b_gate_up, w_down, b_down, combine.T))
    return acc.reshape(B, S, D).astype(x.dtype)


def setup_inputs(seed: int = 0) -> dict:
    key = jax.random.key(seed)
    ks = jax.random.split(key, 20)
    n = jax.random.normal
    f32 = jnp.float32
    return {
        "x": n(ks[0], (BATCH, SEQ, D_MODEL), f32),
        "w_in": n(ks[1], (DEPTH, D_MODEL, D_IN), f32) * D_MODEL ** -0.5,
        "hg_lb_logits": n(ks[2], (DEPTH + 1, HG_KEY), f32) * 0.5,
        "hg_norm_g": 1.0 + 0.02 * n(ks[3], (DEPTH, HG_HEADS, HG_DV), f32),
        "da_lambda": 0.1 * n(ks[4], (DEPTH, 4, DA_HD), f32),
        "da_norm_g": 1.0 + 0.02 * n(ks[5], (DEPTH, DA_HEADS, DA_DV), f32),
        "w_branch_a": n(ks[6], (DEPTH, HG_WIDTH, D_MODEL), f32) * HG_WIDTH ** -0.5,
        "w_branch_b": n(ks[7], (DEPTH, DA_WIDTH, D_MODEL), f32) * DA_WIDTH ** -0.5,
        "w_out": n(ks[8], (DEPTH, D_MODEL, D_MODEL), f32) * (D_MODEL ** -0.5) * DN_BETA,
        "ln1_g": 1.0 + 0.02 * n(ks[9], (DEPTH, D_MODEL), f32),
        "ln1_b": 0.02 * n(ks[10], (DEPTH, D_MODEL), f32),
        "router_w": n(ks[11], (DEPTH, D_MODEL, N_EXPERTS), f32) * D_MODEL ** -0.5,
        "router_b": 0.01 * n(ks[12], (DEPTH, N_EXPERTS), f32),
        "w_gate_up": n(ks[13], (DEPTH, N_EXPERTS, D_MODEL, 2 * D_FF), f32) * D_MODEL ** -0.5,
        "b_gate_up": 0.01 * n(ks[14], (DEPTH, N_EXPERTS, 2 * D_FF), f32),
        "w_down": n(ks[15], (DEPTH, N_EXPERTS, D_FF, D_MODEL), f32) * (D_FF ** -0.5) * DN_BETA,
        "b_down": 0.01 * n(ks[16], (DEPTH, N_EXPERTS, D_MODEL), f32),
        "ln2_g": 1.0 + 0.02 * n(ks[17], (DEPTH, D_MODEL), f32),
        "ln2_b": 0.02 * n(ks[18], (DEPTH, D_MODEL), f32),
    }


def reference(x, w_in, hg_lb_logits, hg_norm_g, da_lambda, da_norm_g, w_branch_a, w_branch_b,
              w_out, ln1_g, ln1_b, router_w, router_b, w_gate_up, b_gate_up, w_down, b_down,
              ln2_g, ln2_b):
    split_points = [int(s) for s in np.cumsum(IN_SIZES)[:-1]]
    lb_all = jnp.cumsum(jax.nn.softmax(hg_lb_logits.astype(jnp.float32), axis=0), axis=0)
    h = x
    for l in range(DEPTH):
        lambda_init = 0.8 - 0.6 * math.exp(-0.3 * l)
        proj = h @ w_in[l]
        hq, hf, hi, hg, dq, dk, dv, ga, gb = jnp.split(proj, split_points, axis=-1)
        o_a = hgrn2_branch(hq, hf, hi, hg, lb_all[l], hg_norm_g[l])
        o_b = diff_attention_branch(dq, dk, dv, da_lambda[l], lambda_init, da_norm_g[l])
        merged = jax.nn.sigmoid(ga) * (o_a @ w_branch_a[l]) + jax.nn.sigmoid(gb) * (o_b @ w_branch_b[l])
        mix = merged @ w_out[l]
        h = layer_norm(DN_ALPHA * h + mix, ln1_g[l], ln1_b[l])
        ffn = moe_ffn(h, router_w[l], router_b[l], w_gate_up[l], b_gate_up[l], w_down[l], b_down[l])
        h = layer_norm(DN_ALPHA * h + ffn, ln2_g[l], ln2_b[l])
    return h
```

```python
import functools
import math

import jax
import jax.numpy as jnp
import numpy as np
from jax import lax
from jax.experimental import pallas as pl
from jax.experimental.pallas import tpu as pltpu

f32 = jnp.float32
bf16 = jnp.bfloat16
i32 = jnp.int32

D_MODEL = 1024
DEPTH = 1
HG_HEADS = 4
HG_DK = 128
HG_DV = 128
HG_KEY = HG_HEADS * HG_DK
HG_WIDTH = HG_HEADS * HG_DV
DA_HEADS = 4
DA_HD = 64
DA_DV = 2 * DA_HD
DA_QK = DA_HEADS * 2 * DA_HD
DA_WIDTH = DA_HEADS * DA_DV
N_EXPERTS = 32
TOP_K = 4
D_FF = 1024
SWIGLU_LIMIT = 7.0
SWIGLU_ALPHA = 1.702
DN_ALPHA = (2.0 * DEPTH) ** 0.25
LN_EPS = 1e-5
NORM_EPS = 1e-6
LAMBDA_INIT = 0.8 - 0.6 * math.exp(-0.3 * 0)

VMEM_LIMIT_BYTES = 52 * 1024 * 1024

PROJ_TM = 1024
PROJ_TN = 512
HG_CHUNK = 128
HG_ROWS = 512
ATT_T = 256
MERGE_TM = 512
MOE_T = 256
ROW_TM = 256
FFN_FC = 512


def _sigmoid(x):
    return 1.0 / (1.0 + jnp.exp(-x))


def _dot(a, b):
    return jnp.dot(a, b, preferred_element_type=f32)


def _dot_nt(a, b):
    return lax.dot_general(a, b, (((1,), (1,)), ((), ())), preferred_element_type=f32)


def _dot_tn(a, b):
    return lax.dot_general(a, b, (((0,), (0,)), ((), ())), preferred_element_type=f32)


def _inproj_kernel(x_ref, w_ref, o_ref, xb_ref):
    @pl.when(pl.program_id(1) == 0)
    def _():
        xb_ref[...] = x_ref[...].astype(bf16)

    o_ref[...] = _dot(xb_ref[...], w_ref[...]).astype(o_ref.dtype)


def _inproj(x2d, w, out_dtype):
    n, k = x2d.shape
    m = w.shape[1]
    tm = min(PROJ_TM, n)
    return pl.pallas_call(
        _inproj_kernel,
        grid=(n // tm, m // PROJ_TN),
        in_specs=[pl.BlockSpec((tm, k), lambda i, j: (i, 0)),
                  pl.BlockSpec((k, PROJ_TN), lambda i, j: (0, j))],
        out_specs=pl.BlockSpec((tm, PROJ_TN), lambda i, j: (i, j)),
        out_shape=jax.ShapeDtypeStruct((n, m), out_dtype),
        scratch_shapes=[pltpu.VMEM((tm, k), bf16)],
        compiler_params=pltpu.CompilerParams(
            dimension_semantics=("parallel", "arbitrary"), vmem_limit_bytes=VMEM_LIMIT_BYTES),
        name="inproj",
    )(x2d, w)


def _shift_down(x, d):
    n = x.shape[0]
    if d % 8 == 0:
        return jnp.concatenate([x[n - d:], x[:n - d]], axis=0)
    return pltpu.roll(x, d, axis=0)


def _shift_up(x, d):
    n = x.shape[0]
    if d % 8 == 0:
        return jnp.concatenate([x[d:], x[:d]], axis=0)
    return pltpu.roll(x, n - d, axis=0)


def _hgrn_chunk_head(qv, fl, v, g, lb, ng, state, tril, lvl, row, eye):
    c = qv.shape[0]
    f = lb + (1.0 - lb) * _sigmoid(fl)
    logf = jnp.log(f)
    kk = 1.0 - f
    qa = qv * _sigmoid(qv)

    l1 = logf.astype(bf16)
    r1 = logf - l1.astype(f32)
    l2 = r1.astype(bf16)
    l3 = (r1 - l2.astype(f32)).astype(bf16)
    bc = _dot(tril, jnp.concatenate([l1, l2, l3], axis=1))
    b = bc[:, 0:HG_DK] + bc[:, HG_DK:2 * HG_DK] + bc[:, 2 * HG_DK:3 * HG_DK]

    attn = jnp.where(lvl == -1, _dot_nt(qa.astype(bf16), kk.astype(bf16)), 0.0)
    filled = b
    d = 1
    level = 0
    while d < c:
        upper = (row & d) != 0
        ref_b = jnp.where(upper, _shift_down(filled, d), filled)
        diff = b - ref_b
        e = jnp.exp(jnp.where(upper, diff, -diff))
        mixed = (jnp.where(upper, qa, kk) * e).astype(bf16)
        attn = jnp.where(lvl == level, _dot_nt(mixed, mixed), attn)
        filled = jnp.where(upper, filled, _shift_up(filled, d))
        d *= 2
        level += 1

    o_intra = _dot(attn.astype(bf16), v)
    o_inter = _dot((qa * jnp.exp(b)).astype(bf16), state.astype(bf16))

    b_last = b[c - 1:c, :]
    k_dec = (kk * jnp.exp(b_last - b)).astype(bf16)
    dcol = jnp.sum(jnp.where(eye, jnp.broadcast_to(jnp.exp(b_last), eye.shape), 0.0), axis=1, keepdims=True)
    new_state = dcol * state + _dot_tn(k_dec, v)

    o = o_inter + o_intra
    ms = jnp.mean(o * o, axis=1, keepdims=True)
    o = o * lax.rsqrt(ms + NORM_EPS) * ng
    o = o * _sigmoid(g.astype(f32))
    return o.astype(bf16), new_state


def _hgrn_kernel(q_ref, f_ref, i_ref, g_ref, lbl_ref, ng_ref, tril_ref, lvl_ref, o_ref, st_ref, *, chunk, n_chunks):
    @pl.when(pl.program_id(1) == 0)
    def _():
        st_ref[...] = jnp.zeros_like(st_ref)

    lbl = lbl_ref[...]
    ex = jnp.exp(lbl - jnp.max(lbl, axis=0, keepdims=True))
    lb_all = ex[0:1, :] / jnp.sum(ex, axis=0, keepdims=True)
    ng_all = ng_ref[...]
    tril = tril_ref[...]
    lvl = lvl_ref[...]
    row = lax.broadcasted_iota(i32, (chunk, HG_DK), 0)
    eye = lax.broadcasted_iota(i32, (HG_DK, HG_DV), 0) == lax.broadcasted_iota(i32, (HG_DK, HG_DV), 1)

    def body(ci, carry):
        r0 = pl.multiple_of(ci * chunk, chunk)
        for h in range(HG_HEADS):
            cs = pl.ds(h * HG_DK, HG_DK)
            out, new_state = _hgrn_chunk_head(
                q_ref[pl.ds(r0, chunk), cs], f_ref[pl.ds(r0, chunk), cs],
                i_ref[pl.ds(r0, chunk), cs], g_ref[pl.ds(r0, chunk), cs],
                lb_all[:, h * HG_DK:(h + 1) * HG_DK], ng_all[:, h * HG_DV:(h + 1) * HG_DV],
                st_ref[h], tril, lvl, row, eye)
            o_ref[pl.ds(r0, chunk), cs] = out
            st_ref[h] = new_state
        return carry

    lax.fori_loop(0, n_chunks, body, 0)


def _hgrn_consts(chunk):
    t = np.arange(chunk)
    tril = (t[None, :] <= t[:, None]).astype(np.float32)
    x = t[:, None] ^ t[None, :]
    lvl = np.where(x > 0, np.floor(np.log2(np.maximum(x, 1))).astype(np.int32), -1)
    lvl = np.where(t[:, None] >= t[None, :], lvl, -2).astype(np.int32)
    return jnp.asarray(tril, dtype=bf16), jnp.asarray(lvl, dtype=i32)


def _hgrn(pf, pb, lb_logits, norm_g, batch, seq):
    n = batch * seq
    rows = min(HG_ROWS, seq)
    chunk = min(HG_CHUNK, rows)
    spb = seq // rows
    tril, lvl = _hgrn_consts(chunk)
    row_blk = lambda b, s: b * spb + s
    return pl.pallas_call(
        functools.partial(_hgrn_kernel, chunk=chunk, n_chunks=rows // chunk),
        grid=(batch, spb),
        in_specs=[
            pl.BlockSpec((rows, HG_KEY), lambda b, s: (row_blk(b, s), 0)),
            pl.BlockSpec((rows, HG_KEY), lambda b, s: (row_blk(b, s), 1)),
            pl.BlockSpec((rows, HG_WIDTH), lambda b, s: (row_blk(b, s), 4)),
            pl.BlockSpec((rows, HG_WIDTH), lambda b, s: (row_blk(b, s), 5)),
            pl.BlockSpec(lb_logits.shape, lambda b, s: (0, 0)),
            pl.BlockSpec((1, HG_WIDTH), lambda b, s: (0, 0)),
            pl.BlockSpec((chunk, chunk), lambda b, s: (0, 0)),
            pl.BlockSpec((chunk, chunk), lambda b, s: (0, 0)),
        ],
        out_specs=pl.BlockSpec((rows, HG_WIDTH), lambda b, s: (row_blk(b, s), 0)),
        out_shape=jax.ShapeDtypeStruct((n, HG_WIDTH), bf16),
        scratch_shapes=[pltpu.VMEM((HG_HEADS, HG_DK, HG_DV), f32)],
        compiler_params=pltpu.CompilerParams(
            dimension_semantics=("parallel", "arbitrary"), vmem_limit_bytes=VMEM_LIMIT_BYTES),
        name="hgrn2",
    )(pf, pf, pb, pb, lb_logits, norm_g, tril, lvl)


def _attn_kernel(lam_ref, ng_ref, q_ref, k_ref, v_ref, o_ref, qs_ref, m_ref, l_ref, acc_ref, *, t):
    h = pl.program_id(1)
    qi = pl.program_id(2)

    lane = lax.broadcasted_iota(i32, (t, 2 * DA_HD), 1)
    q = q_ref[...] * jnp.asarray(DA_HD ** -0.5, bf16)
    zero = jnp.zeros_like(q)
    qs_ref[0:t, :] = jnp.where(lane < DA_HD, q, zero)
    qs_ref[t:2 * t, :] = jnp.where(lane >= DA_HD, q, zero)
    m_ref[...] = jnp.full_like(m_ref, -jnp.inf)
    l_ref[...] = jnp.zeros_like(l_ref)
    acc_ref[...] = jnp.zeros_like(acc_ref)

    slope = jnp.exp2(jnp.full((1, t), -8.0 / DA_HEADS, f32) * (h + 1).astype(f32))
    col = lax.broadcasted_iota(i32, (1, t), 1)

    def step(kt, masked):
        k0 = pl.multiple_of(kt * t, t)
        k = k_ref[pl.ds(k0, t), :]
        v = v_ref[pl.ds(k0, t), :]
        s = _dot_nt(qs_ref[...], k)
        s = s + slope * ((kt - qi) * t + col).astype(f32)
        if masked:
            rr = lax.broadcasted_iota(i32, (2 * t, t), 0)
            rr = jnp.where(rr >= t, rr - t, rr)
            cc = lax.broadcasted_iota(i32, (2 * t, t), 1)
            s = jnp.where(cc <= rr, s, -jnp.inf)
        m_old = m_ref[...]
        m_new = jnp.maximum(m_old, jnp.max(s, axis=1, keepdims=True))
        alpha = jnp.exp(m_old - m_new)
        p = jnp.exp(s - m_new[:, 0:1])
        l_ref[...] = alpha * l_ref[...] + jnp.sum(p, axis=1, keepdims=True)
        acc_ref[...] = alpha * acc_ref[...] + _dot(p.astype(bf16), v)
        m_ref[...] = m_new

    def loop_body(kt, carry):
        step(kt, False)
        return carry

    lax.fori_loop(0, qi, loop_body, 0)
    step(qi, True)

    lp = lam_ref[...]
    lam = (jnp.exp(jnp.sum(lp[0:1, :] * lp[1:2, :], axis=1, keepdims=True))
           - jnp.exp(jnp.sum(lp[2:3, :] * lp[3:4, :], axis=1, keepdims=True)) + LAMBDA_INIT)
    o_all = acc_ref[...] / l_ref[...]
    o = o_all[0:t, :] - lam * o_all[t:2 * t, :]
    ms = jnp.mean(o * o, axis=1, keepdims=True)
    o = o * lax.rsqrt(ms + NORM_EPS) * ng_ref[0] * (1.0 - LAMBDA_INIT)
    o_ref[...] = o.astype(bf16)


def _attn(pb, lam_params, norm_g, batch, seq):
    n = batch * seq
    t = min(ATT_T, seq)
    nq = seq // t
    blk = 2 * DA_HD
    q0, k0, v0 = 3072 // blk, 3584 // blk, 4096 // blk
    return pl.pallas_call(
        functools.partial(_attn_kernel, t=t),
        grid=(batch, DA_HEADS, nq),
        in_specs=[
            pl.BlockSpec(lam_params.shape, lambda b, h, i: (0, 0)),
            pl.BlockSpec((1, 1, DA_DV), lambda b, h, i: (h, 0, 0)),
            pl.BlockSpec((t, blk), lambda b, h, i: (b * nq + i, q0 + h)),
            pl.BlockSpec((seq, blk), lambda b, h, i: (b, k0 + h)),
            pl.BlockSpec((seq, blk), lambda b, h, i: (b, v0 + h)),
        ],
        out_specs=pl.BlockSpec((t, DA_DV), lambda b, h, i: (b * nq + i, h)),
        out_shape=jax.ShapeDtypeStruct((n, DA_WIDTH), bf16),
        scratch_shapes=[pltpu.VMEM((2 * t, blk), bf16), pltpu.VMEM((2 * t, DA_DV), f32),
                        pltpu.VMEM((2 * t, DA_DV), f32), pltpu.VMEM((2 * t, DA_DV), f32)],
        compiler_params=pltpu.CompilerParams(
            dimension_semantics=("parallel", "parallel", "arbitrary"), vmem_limit_bytes=VMEM_LIMIT_BYTES),
        name="diff_attn",
    )(lam_params, norm_g.reshape(DA_HEADS, 1, DA_DV), pb, pb, pb)


def _layer_norm(y, g, b):
    mu = jnp.mean(y, axis=1, keepdims=True)
    yc = y - mu
    var = jnp.mean(yc * yc, axis=1, keepdims=True)
    return yc * lax.rsqrt(var + LN_EPS) * g + b


def _assemble4(cols, dtype):
    tm = cols[0].shape[0]
    lane = lax.broadcasted_iota(i32, (tm, TOP_K), 1)
    out = jnp.broadcast_to(cols[TOP_K - 1], (tm, TOP_K))
    for k in range(TOP_K - 2, -1, -1):
        out = jnp.where(lane == k, jnp.broadcast_to(cols[k], (tm, TOP_K)), out)
    return out.astype(dtype)


def _merge_kernel(oa_ref, ob_ref, ga_ref, gb_ref, x_ref, wa_ref, wb_ref, wo_ref, g1_ref, b1_ref, rw_ref, rb_ref,
                  tri_ref, h1_ref, idx_ref, tw_ref, rank_ref, cnt_ref, carry_ref):
    @pl.when(pl.program_id(0) == 0)
    def _():
        carry_ref[...] = jnp.zeros_like(carry_ref)

    a = _dot(oa_ref[...], wa_ref[...])
    b = _dot(ob_ref[...], wb_ref[...])
    merged = _sigmoid(ga_ref[...].astype(f32)) * a + _sigmoid(gb_ref[...].astype(f32)) * b
    mix = _dot(merged.astype(bf16), wo_ref[...])
    h1 = _layer_norm(DN_ALPHA * x_ref[...] + mix, g1_ref[...], b1_ref[...])
    h1_ref[...] = h1

    logits = _dot(h1.astype(bf16), rw_ref[...]) + rb_ref[...]
    tm = logits.shape[0]
    lane = lax.broadcasted_iota(i32, (tm, N_EXPERTS), 1).astype(f32)
    work = logits
    vals, idxs = [], []
    for _ in range(TOP_K):
        mk = jnp.max(work, axis=1, keepdims=True)
        ik = jnp.min(jnp.where(work == mk, lane, float(N_EXPERTS)), axis=1, keepdims=True)
        vals.append(mk)
        idxs.append(ik)
        work = jnp.where(lane == ik, -jnp.inf, work)
    es = [jnp.exp(v - vals[0]) for v in vals]
    den = es[0] + es[1] + es[2] + es[3]
    tw_ref[...] = _assemble4([e / den for e in es], f32)
    idx_ref[...] = _assemble4(idxs, i32)

    onehot = jnp.zeros((tm, N_EXPERTS), f32)
    for ik in idxs:
        onehot = onehot + jnp.where(lane == ik, 1.0, 0.0)
    before = _dot(tri_ref[...], onehot.astype(bf16)) + carry_ref[...]
    ranks = [jnp.sum(jnp.where(lane == ik, before, 0.0), axis=1, keepdims=True) for ik in idxs]
    rank_ref[...] = _assemble4(ranks, i32)
    total = carry_ref[...] + jnp.sum(onehot, axis=0, keepdims=True)
    carry_ref[...] = total
    cnt_ref[...] = total


def _merge(o_a, o_b, pb, x2d, wa, wb, wo, g1, b1, rw, rb):
    n = x2d.shape[0]
    tm = min(MERGE_TM, n)
    t = np.arange(tm)
    tri = jnp.asarray((t[None, :] < t[:, None]).astype(np.float32), dtype=bf16)
    row = lambda i: (i, 0)
    const = lambda i: (0, 0)
    return pl.pallas_call(
        _merge_kernel,
        grid=(n // tm,),
        in_specs=[
            pl.BlockSpec((tm, HG_WIDTH), row),
            pl.BlockSpec((tm, DA_WIDTH), row),
            pl.BlockSpec((tm, D_MODEL), lambda i: (i, 0)),
            pl.BlockSpec((tm, D_MODEL), lambda i: (i, 1)),
            pl.BlockSpec((tm, D_MODEL), row),
            pl.BlockSpec(wa.shape, const), pl.BlockSpec(wb.shape, const), pl.BlockSpec(wo.shape, const),
            pl.BlockSpec(g1.shape, const), pl.BlockSpec(b1.shape, const),
            pl.BlockSpec(rw.shape, const), pl.BlockSpec(rb.shape, const),
            pl.BlockSpec((tm, tm), const),
        ],
        out_specs=[
            pl.BlockSpec((tm, D_MODEL), row),
            pl.BlockSpec((tm, TOP_K), row), pl.BlockSpec((tm, TOP_K), row), pl.BlockSpec((tm, TOP_K), row),
            pl.BlockSpec((1, N_EXPERTS), const),
        ],
        out_shape=[
            jax.ShapeDtypeStruct((n, D_MODEL), f32),
            jax.ShapeDtypeStruct((n, TOP_K), i32), jax.ShapeDtypeStruct((n, TOP_K), f32),
            jax.ShapeDtypeStruct((n, TOP_K), i32),
            jax.ShapeDtypeStruct((1, N_EXPERTS), f32),
        ],
        scratch_shapes=[pltpu.VMEM((1, N_EXPERTS), f32)],
        compiler_params=pltpu.CompilerParams(
            dimension_semantics=("arbitrary",), vmem_limit_bytes=VMEM_LIMIT_BYTES),
        name="merge_ln_router",
    )(o_a, o_b, pb, pb, x2d, wa, wb, wo, g1, b1, rw, rb, tri)


def _row_copy_loop(n_rows, start_one):
    def body(t, carry):
        for k in range(TOP_K):
            start_one(t, k)
        return carry
    lax.fori_loop(0, n_rows, body, 0)


def _dispatch_kernel(zrow_ref, zflag_ref, pos_ref, h1_ref, xs_ref, zeros_ref, sem, zsem):
    tm = h1_ref.shape[0]

    @pl.when(pl.program_id(0) == 0)
    def _():
        zeros_ref[...] = jnp.zeros_like(zeros_ref)
        for e in range(2 * N_EXPERTS):
            @pl.when(zflag_ref[e] == 1)
            def _():
                z0 = pl.multiple_of(zrow_ref[e], MOE_T)
                cp = pltpu.make_async_copy(zeros_ref, xs_ref.at[pl.ds(z0, MOE_T)], zsem)
                cp.start()
                cp.wait()

    def row_copy(t, k):
        return pltpu.make_async_copy(h1_ref.at[pl.ds(t, 1)], xs_ref.at[pl.ds(pos_ref[t * TOP_K + k], 1)], sem)

    _row_copy_loop(tm, lambda t, k: row_copy(t, k).start())
    _row_copy_loop(tm, lambda t, k: row_copy(t, k).wait())


def _dispatch(h1, pos_flat, zrow, zflag, n_rows_sorted):
    n = h1.shape[0]
    tm = min(ROW_TM, n)
    return pl.pallas_call(
        _dispatch_kernel,
        grid_spec=pltpu.PrefetchScalarGridSpec(
            num_scalar_prefetch=2,
            grid=(n // tm,),
            in_specs=[
                pl.BlockSpec((tm * TOP_K,), lambda i, zr, zf: (i,), memory_space=pltpu.SMEM),
                pl.BlockSpec((tm, D_MODEL), lambda i, zr, zf: (i, 0)),
            ],
            out_specs=pl.BlockSpec(memory_space=pl.ANY),
            scratch_shapes=[pltpu.VMEM((MOE_T, D_MODEL), f32), pltpu.SemaphoreType.DMA, pltpu.SemaphoreType.DMA],
        ),
        out_shape=jax.ShapeDtypeStruct((n_rows_sorted, D_MODEL), f32),
        compiler_params=pltpu.CompilerParams(
            dimension_semantics=("arbitrary",), vmem_limit_bytes=VMEM_LIMIT_BYTES),
        name="moe_dispatch",
    )(zrow, zflag, pos_flat, h1)


def _ffn_kernel(te_ref, nt_ref, x_ref, wg_ref, wu_ref, wd_ref, bg_ref, bu_ref, bd_ref, y_ref):
    @pl.when(pl.program_id(0) < nt_ref[0])
    def _():
        x = x_ref[...].astype(bf16)
        acc = jnp.broadcast_to(bd_ref[0], y_ref.shape)
        for c0 in range(0, D_FF, FFN_FC):
            cs = slice(c0, c0 + FFN_FC)
            hg = _dot(x, wg_ref[0, :, cs]) + bg_ref[0, :, cs]
            hu = _dot(x, wu_ref[0, :, cs]) + bu_ref[0, :, cs]
            gate = jnp.minimum(hg, SWIGLU_LIMIT)
            up = jnp.clip(hu, -SWIGLU_LIMIT, SWIGLU_LIMIT)
            glu = gate * _sigmoid(gate * SWIGLU_ALPHA)
            acc = acc + _dot(((up + 1.0) * glu).astype(bf16), wd_ref[0, cs, :])
        y_ref[...] = acc

    @pl.when(pl.program_id(0) >= nt_ref[0])
    def _():
        y_ref[...] = jnp.zeros_like(y_ref)


def _ffn(xs, tile_e, n_tiles, wg, wu, wd, bg, bu, bd):
    p = xs.shape[0]
    last = lambda j, te, nt: jnp.minimum(j, nt[0] - 1)
    wmap = lambda j, te, nt: (te[j], 0, 0)
    return pl.pallas_call(
        _ffn_kernel,
        grid_spec=pltpu.PrefetchScalarGridSpec(
            num_scalar_prefetch=2,
            grid=(p // MOE_T,),
            in_specs=[
                pl.BlockSpec((MOE_T, D_MODEL), lambda j, te, nt: (last(j, te, nt), 0)),
                pl.BlockSpec((1, D_MODEL, D_FF), wmap), pl.BlockSpec((1, D_MODEL, D_FF), wmap),
                pl.BlockSpec((1, D_FF, D_MODEL), wmap),
                pl.BlockSpec((1, 1, D_FF), wmap), pl.BlockSpec((1, 1, D_FF), wmap),
                pl.BlockSpec((1, 1, D_MODEL), wmap),
            ],
            out_specs=pl.BlockSpec((MOE_T, D_MODEL), lambda j, te, nt: (j, 0)),
        ),
        out_shape=jax.ShapeDtypeStruct((p, D_MODEL), f32),
        compiler_params=pltpu.CompilerParams(
            dimension_semantics=("arbitrary",), vmem_limit_bytes=VMEM_LIMIT_BYTES),
        name="moe_ffn",
    )(tile_e, n_tiles, xs, wg, wu, wd, bg, bu, bd)


def _combine_kernel(pos_ref, h1_ref, tw_ref, g2_ref, b2_ref, ys_ref, o_ref, ybuf_ref, sem):
    tm = h1_ref.shape[0]

    def row_copy(t, k):
        return pltpu.make_async_copy(ys_ref.at[pl.ds(pos_ref[t * TOP_K + k], 1)], ybuf_ref.at[k, pl.ds(t, 1)], sem)

    _row_copy_loop(tm, lambda t, k: row_copy(t, k).start())
    _row_copy_loop(tm, lambda t, k: row_copy(t, k).wait())

    tw = tw_ref[...]
    ffn = tw[:, 0:1] * ybuf_ref[0]
    for k in range(1, TOP_K):
        ffn = ffn + tw[:, k:k + 1] * ybuf_ref[k]
    o_ref[...] = _layer_norm(DN_ALPHA * h1_ref[...] + ffn, g2_ref[...], b2_ref[...])


def _combine(h1, tw, pos_flat, ys, g2, b2):
    n = h1.shape[0]
    tm = min(ROW_TM, n)
    return pl.pallas_call(
        _combine_kernel,
        grid=(n // tm,),
        in_specs=[
            pl.BlockSpec((tm * TOP_K,), lambda i: (i,), memory_space=pltpu.SMEM),
            pl.BlockSpec((tm, D_MODEL), lambda i: (i, 0)),
            pl.BlockSpec((tm, TOP_K), lambda i: (i, 0)),
            pl.BlockSpec(g2.shape, lambda i: (0, 0)), pl.BlockSpec(b2.shape, lambda i: (0, 0)),
            pl.BlockSpec(memory_space=pl.ANY),
        ],
        out_specs=pl.BlockSpec((tm, D_MODEL), lambda i: (i, 0)),
        out_shape=jax.ShapeDtypeStruct((n, D_MODEL), f32),
        scratch_shapes=[pltpu.VMEM((TOP_K, tm, D_MODEL), f32), pltpu.SemaphoreType.DMA],
        compiler_params=pltpu.CompilerParams(
            dimension_semantics=("arbitrary",), vmem_limit_bytes=VMEM_LIMIT_BYTES),
        name="moe_combine_ln",
    )(pos_flat, h1, tw, g2, b2, ys)


def _routing_tables(counts, idx, rank, n_tiles_max):
    padded = ((counts + MOE_T - 1) // MOE_T) * MOE_T
    ends = jnp.cumsum(padded)
    offs = ends - padded
    pos = (offs[idx] + rank).reshape(-1)
    n_tiles = (ends[-1] // MOE_T).astype(i32)
    tile_start = jnp.minimum(jnp.arange(n_tiles_max, dtype=i32), n_tiles - 1) * MOE_T
    tile_e = jnp.minimum(jnp.searchsorted(ends, tile_start, side="right"), N_EXPERTS - 1).astype(i32)
    trailing = jnp.arange(n_tiles_max - N_EXPERTS, n_tiles_max, dtype=i32)
    zrow = jnp.concatenate([(ends - MOE_T).astype(i32), trailing * MOE_T])
    zflag = jnp.concatenate([(padded > counts).astype(i32), (trailing >= n_tiles).astype(i32)])
    return pos.astype(i32), tile_e, n_tiles.reshape(1), zrow, zflag


def kernel(x, w_in, hg_lb_logits, hg_norm_g, da_lambda, da_norm_g, w_branch_a, w_branch_b, w_out, ln1_g, ln1_b,
           router_w, router_b, w_gate_up, b_gate_up, w_down, b_down, ln2_g, ln2_b):
    batch, seq, d = x.shape
    assert d == D_MODEL and w_in.shape[0] == DEPTH == 1
    n = batch * seq
    x2d = x.reshape(n, d)

    w = w_in[0]
    w_f = w[:, 0:2 * HG_KEY].astype(bf16)
    o_hi, o_dq, o_ga = 2 * HG_KEY, 2 * HG_KEY + 2 * HG_WIDTH, 2 * HG_KEY + 2 * HG_WIDTH + 2 * DA_QK + DA_WIDTH
    w_b = jnp.concatenate([w[:, o_ga:], w[:, o_hi:o_ga]], axis=1).astype(bf16)
    pf = _inproj(x2d, w_f, f32)
    pb = _inproj(x2d, w_b, bf16)

    o_a = _hgrn(pf, pb, hg_lb_logits, hg_norm_g[0].reshape(1, HG_WIDTH), batch, seq)
    o_b = _attn(pb, da_lambda[0], da_norm_g[0], batch, seq)

    h1, idx, tw, rank, cnt = _merge(
        o_a, o_b, pb, x2d, w_branch_a[0].astype(bf16), w_branch_b[0].astype(bf16), w_out[0].astype(bf16),
        ln1_g[0].reshape(1, d), ln1_b[0].reshape(1, d), router_w[0].astype(bf16), router_b[0].reshape(1, N_EXPERTS))

    n_rows_sorted = n * TOP_K + N_EXPERTS * MOE_T
    pos, tile_e, n_tiles, zrow, zflag = _routing_tables(cnt[0].astype(i32), idx, rank, n_rows_sorted // MOE_T)

    xs = _dispatch(h1, pos, zrow, zflag, n_rows_sorted)
    wgu = w_gate_up[0]
    ys = _ffn(xs, tile_e, n_tiles,
              wgu[:, :, 0::2].astype(bf16), wgu[:, :, 1::2].astype(bf16), w_down[0].astype(bf16),
              b_gate_up[0][:, None, 0::2], b_gate_up[0][:, None, 1::2], b_down[0][:, None, :])
    out = _combine(h1, tw, pos, ys, ln2_g[0].reshape(1, d), ln2_b[0].reshape(1, d))
    return out.reshape(batch, seq, d)
```

```python
import functools
import math

import jax
import jax.numpy as jnp
import numpy as np
from jax import lax
from jax.experimental import pallas as pl
from jax.experimental.pallas import tpu as pltpu

f32 = jnp.float32
bf16 = jnp.bfloat16
i32 = jnp.int32

D_MODEL = 1024
DEPTH = 1
HG_HEADS = 4
HG_DK = 128
HG_DV = 128
HG_KEY = HG_HEADS * HG_DK
HG_WIDTH = HG_HEADS * HG_DV
DA_HEADS = 4
DA_HD = 64
DA_DV = 2 * DA_HD
DA_QK = DA_HEADS * 2 * DA_HD
DA_WIDTH = DA_HEADS * DA_DV
N_EXPERTS = 32
TOP_K = 4
D_FF = 1024
SWIGLU_LIMIT = 7.0
SWIGLU_ALPHA = 1.702
DN_ALPHA = (2.0 * DEPTH) ** 0.25
LN_EPS = 1e-5
NORM_EPS = 1e-6
LAMBDA_INIT = 0.8 - 0.6 * math.exp(-0.3 * 0)

VMEM_LIMIT_BYTES = 52 * 1024 * 1024

PROJ_TM = 1024
PROJ_TN = 512
HG_CHUNK = 128
HG_ROWS = 512
ATT_T = 256
ATT_R = 128
MERGE_TM = 512
MOE_T = 256
ROW_TM = 256
FFN_FC = 512
LANES = 128


def _sigmoid(x):
    return 1.0 / (1.0 + jnp.exp(-x))


def _dot(a, b):
    return jnp.dot(a, b, preferred_element_type=f32)


def _dot_nt(a, b):
    return lax.dot_general(a, b, (((1,), (1,)), ((), ())), preferred_element_type=f32)


def _dot_tn(a, b):
    return lax.dot_general(a, b, (((0,), (0,)), ((), ())), preferred_element_type=f32)


def _inproj_kernel(x_ref, w_ref, o_ref, xb_ref):
    @pl.when(pl.program_id(1) == 0)
    def _():
        xb_ref[...] = x_ref[...].astype(bf16)

    o_ref[...] = _dot(xb_ref[...], w_ref[...]).astype(o_ref.dtype)


def _inproj(x2d, w, out_dtype):
    n, k = x2d.shape
    m = w.shape[1]
    tm = min(PROJ_TM, n)
    return pl.pallas_call(
        _inproj_kernel,
        grid=(n // tm, m // PROJ_TN),
        in_specs=[pl.BlockSpec((tm, k), lambda i, j: (i, 0)),
                  pl.BlockSpec((k, PROJ_TN), lambda i, j: (0, j))],
        out_specs=pl.BlockSpec((tm, PROJ_TN), lambda i, j: (i, j)),
        out_shape=jax.ShapeDtypeStruct((n, m), out_dtype),
        scratch_shapes=[pltpu.VMEM((tm, k), bf16)],
        compiler_params=pltpu.CompilerParams(
            dimension_semantics=("parallel", "arbitrary"), vmem_limit_bytes=VMEM_LIMIT_BYTES),
        name="inproj",
    )(x2d, w)


def _shift_down(x, d):
    n = x.shape[0]
    if d % 8 == 0:
        return jnp.concatenate([x[n - d:], x[:n - d]], axis=0)
    return pltpu.roll(x, d, axis=0)


def _shift_up(x, d):
    n = x.shape[0]
    if d % 8 == 0:
        return jnp.concatenate([x[d:], x[:d]], axis=0)
    return pltpu.roll(x, n - d, axis=0)


def _hgrn_chunk_head(qv, fl, v, g, lb, ng, state, tril, lvl, row, eye):
    c = qv.shape[0]
    f = lb + (1.0 - lb) * _sigmoid(fl)
    logf = jnp.log(f)
    kk = 1.0 - f
    qa = qv * _sigmoid(qv)

    l1 = logf.astype(bf16)
    r1 = logf - l1.astype(f32)
    l2 = r1.astype(bf16)
    l3 = (r1 - l2.astype(f32)).astype(bf16)
    bc = _dot(tril, jnp.concatenate([l1, l2, l3], axis=1))
    b = bc[:, 0:HG_DK] + bc[:, HG_DK:2 * HG_DK] + bc[:, 2 * HG_DK:3 * HG_DK]

    attn = jnp.where(lvl == -1, _dot_nt(qa.astype(bf16), kk.astype(bf16)), 0.0)
    filled = b
    d = 1
    level = 0
    while d < c:
        upper = (row & d) != 0
        ref_b = jnp.where(upper, _shift_down(filled, d), filled)
        diff = b - ref_b
        e = jnp.exp(jnp.where(upper, diff, -diff))
        mixed = (jnp.where(upper, qa, kk) * e).astype(bf16)
        attn = jnp.where(lvl == level, _dot_nt(mixed, mixed), attn)
        filled = jnp.where(upper, filled, _shift_up(filled, d))
        d *= 2
        level += 1

    o_intra = _dot(attn.astype(bf16), v)
    o_inter = _dot((qa * jnp.exp(b)).astype(bf16), state.astype(bf16))

    b_last = b[c - 1:c, :]
    k_dec = (kk * jnp.exp(b_last - b)).astype(bf16)
    dcol = jnp.sum(jnp.where(eye, jnp.broadcast_to(jnp.exp(b_last), eye.shape), 0.0), axis=1, keepdims=True)
    new_state = dcol * state + _dot_tn(k_dec, v)

    o = o_inter + o_intra
    ms = jnp.mean(o * o, axis=1, keepdims=True)
    o = o * lax.rsqrt(ms + NORM_EPS) * ng
    o = o * _sigmoid(g.astype(f32))
    return o.astype(bf16), new_state


def _hgrn_kernel(q_ref, f_ref, i_ref, g_ref, lbl_ref, ng_ref, tril_ref, lvl_ref, o_ref, st_ref, *, chunk, n_chunks):
    @pl.when(pl.program_id(1) == 0)
    def _():
        st_ref[...] = jnp.zeros_like(st_ref)

    lbl = lbl_ref[...]
    ex = jnp.exp(lbl - jnp.max(lbl, axis=0, keepdims=True))
    lb_all = ex[0:1, :] / jnp.sum(ex, axis=0, keepdims=True)
    ng_all = ng_ref[...]
    tril = tril_ref[...]
    lvl = lvl_ref[...]
    row = lax.broadcasted_iota(i32, (chunk, HG_DK), 0)
    eye = lax.broadcasted_iota(i32, (HG_DK, HG_DV), 0) == lax.broadcasted_iota(i32, (HG_DK, HG_DV), 1)

    def body(ci, carry):
        r0 = pl.multiple_of(ci * chunk, chunk)
        for h in range(HG_HEADS):
            cs = pl.ds(h * HG_DK, HG_DK)
            out, new_state = _hgrn_chunk_head(
                q_ref[pl.ds(r0, chunk), cs], f_ref[pl.ds(r0, chunk), cs],
                i_ref[pl.ds(r0, chunk), cs], g_ref[pl.ds(r0, chunk), cs],
                lb_all[:, h * HG_DK:(h + 1) * HG_DK], ng_all[:, h * HG_DV:(h + 1) * HG_DV],
                st_ref[h], tril, lvl, row, eye)
            o_ref[pl.ds(r0, chunk), cs] = out
            st_ref[h] = new_state
        return carry

    lax.fori_loop(0, n_chunks, body, 0)


def _hgrn_consts(chunk):
    t = np.arange(chunk)
    tril = (t[None, :] <= t[:, None]).astype(np.float32)
    x = t[:, None] ^ t[None, :]
    lvl = np.where(x > 0, np.floor(np.log2(np.maximum(x, 1))).astype(np.int32), -1)
    lvl = np.where(t[:, None] >= t[None, :], lvl, -2).astype(np.int32)
    return jnp.asarray(tril, dtype=bf16), jnp.asarray(lvl, dtype=i32)


def _hgrn(pf, pb, lb_logits, norm_g, batch, seq):
    n = batch * seq
    rows = min(HG_ROWS, seq)
    chunk = min(HG_CHUNK, rows)
    spb = seq // rows
    tril, lvl = _hgrn_consts(chunk)
    row_blk = lambda b, s: b * spb + s
    return pl.pallas_call(
        functools.partial(_hgrn_kernel, chunk=chunk, n_chunks=rows // chunk),
        grid=(batch, spb),
        in_specs=[
            pl.BlockSpec((rows, HG_KEY), lambda b, s: (row_blk(b, s), 0)),
            pl.BlockSpec((rows, HG_KEY), lambda b, s: (row_blk(b, s), 1)),
            pl.BlockSpec((rows, HG_WIDTH), lambda b, s: (row_blk(b, s), 4)),
            pl.BlockSpec((rows, HG_WIDTH), lambda b, s: (row_blk(b, s), 5)),
            pl.BlockSpec(lb_logits.shape, lambda b, s: (0, 0)),
            pl.BlockSpec((1, HG_WIDTH), lambda b, s: (0, 0)),
            pl.BlockSpec((chunk, chunk), lambda b, s: (0, 0)),
            pl.BlockSpec((chunk, chunk), lambda b, s: (0, 0)),
        ],
        out_specs=pl.BlockSpec((rows, HG_WIDTH), lambda b, s: (row_blk(b, s), 0)),
        out_shape=jax.ShapeDtypeStruct((n, HG_WIDTH), bf16),
        scratch_shapes=[pltpu.VMEM((HG_HEADS, HG_DK, HG_DV), f32)],
        compiler_params=pltpu.CompilerParams(
            dimension_semantics=("parallel", "arbitrary"), vmem_limit_bytes=VMEM_LIMIT_BYTES),
        name="hgrn2",
    )(pf, pf, pb, pb, lb_logits, norm_g, tril, lvl)


def _attn_kernel(lam_ref, ng_ref, q_ref, k_ref, v_ref, o_ref, qs_ref, m_ref, acc_ref, *, t):
    qi = pl.program_id(1)
    blk = 2 * DA_HD

    lane = lax.broadcasted_iota(i32, (t, blk), 1)
    for h in range(DA_HEADS):
        q = q_ref[:, h * blk:(h + 1) * blk] * jnp.asarray(DA_HD ** -0.5, bf16)
        zero = jnp.zeros_like(q)
        qs_ref[h, 0:t, :] = jnp.where(lane < DA_HD, q, zero)
        qs_ref[h, t:2 * t, :] = jnp.where(lane >= DA_HD, q, zero)
    m_ref[...] = jnp.full_like(m_ref, -jnp.inf)
    acc_ref[...] = jnp.zeros_like(acc_ref)

    col = lax.broadcasted_iota(i32, (1, t), 1)
    ones = jnp.ones((t, DA_DV), bf16)

    def step(kt, masked):
        k0 = pl.multiple_of(kt * t, t)
        rel = ((kt - qi) * t + col).astype(f32)
        if masked:
            rr = lax.broadcasted_iota(i32, (2 * t, t), 0)
            rr = jnp.where(rr >= t, rr - t, rr)
            causal = lax.broadcasted_iota(i32, (2 * t, t), 1) <= rr
        for h in range(DA_HEADS):
            cs = pl.ds(h * blk, blk)
            k = k_ref[pl.ds(k0, t), cs]
            vo = jnp.concatenate([v_ref[pl.ds(k0, t), cs], ones], axis=1)
            bias = (2.0 ** (-8.0 * (h + 1) / DA_HEADS)) * rel
            for r0 in range(0, 2 * t, ATT_R):
                rows = slice(r0, r0 + ATT_R)
                s = _dot_nt(qs_ref[h, rows, :], k) + bias
                if masked:
                    s = jnp.where(causal[rows, :], s, -jnp.inf)
                m_old = m_ref[h, rows, :]
                m_new = jnp.maximum(m_old, jnp.max(s, axis=1, keepdims=True))
                alpha = jnp.exp(m_old - m_new)
                p = jnp.exp(s - jnp.concatenate([m_new] * (t // DA_DV), axis=1)).astype(bf16)
                acc_ref[h, rows, :] = jnp.concatenate([alpha, alpha], axis=1) * acc_ref[h, rows, :] + _dot(p, vo)
                m_ref[h, rows, :] = m_new

    def loop_body(kt, carry):
        step(kt, False)
        return carry

    lax.fori_loop(0, qi, loop_body, 0)
    step(qi, True)

    lp = lam_ref[...]
    lam = (jnp.exp(jnp.sum(lp[0:1, :] * lp[1:2, :], axis=1, keepdims=True))
           - jnp.exp(jnp.sum(lp[2:3, :] * lp[3:4, :], axis=1, keepdims=True)) + LAMBDA_INIT)
    for h in range(DA_HEADS):
        acc = acc_ref[h]
        o_all = acc[:, 0:DA_DV] / acc[:, DA_DV:2 * DA_DV]
        o = o_all[0:t, :] - lam * o_all[t:2 * t, :]
        ms = jnp.mean(o * o, axis=1, keepdims=True)
        o = o * lax.rsqrt(ms + NORM_EPS) * ng_ref[:, h * DA_DV:(h + 1) * DA_DV] * (1.0 - LAMBDA_INIT)
        o_ref[:, h * DA_DV:(h + 1) * DA_DV] = o.astype(bf16)


def _attn(pb, lam_params, norm_g, batch, seq):
    n = batch * seq
    t = min(ATT_T, seq)
    nq = seq // t
    q0, k0, v0 = 3072 // DA_QK, 3584 // DA_QK, 4096 // DA_WIDTH
    return pl.pallas_call(
        functools.partial(_attn_kernel, t=t),
        grid=(batch, nq),
        in_specs=[
            pl.BlockSpec(lam_params.shape, lambda b, i: (0, 0)),
            pl.BlockSpec((1, DA_WIDTH), lambda b, i: (0, 0)),
            pl.BlockSpec((t, DA_QK), lambda b, i: (b * nq + i, q0)),
            pl.BlockSpec((seq, DA_QK), lambda b, i: (b, k0)),
            pl.BlockSpec((seq, DA_WIDTH), lambda b, i: (b, v0)),
        ],
        out_specs=pl.BlockSpec((t, DA_WIDTH), lambda b, i: (b * nq + i, 0)),
        out_shape=jax.ShapeDtypeStruct((n, DA_WIDTH), bf16),
        scratch_shapes=[pltpu.VMEM((DA_HEADS, 2 * t, 2 * DA_HD), bf16), pltpu.VMEM((DA_HEADS, 2 * t, DA_DV), f32),
                        pltpu.VMEM((DA_HEADS, 2 * t, 2 * DA_DV), f32)],
        compiler_params=pltpu.CompilerParams(
            dimension_semantics=("parallel", "arbitrary"), vmem_limit_bytes=VMEM_LIMIT_BYTES),
        name="diff_attn",
    )(lam_params, norm_g.reshape(1, DA_WIDTH), pb, pb, pb)


def _layer_norm(y, g, b):
    mu = jnp.mean(y, axis=1, keepdims=True)
    yc = y - mu
    var = jnp.mean(yc * yc, axis=1, keepdims=True)
    return yc * lax.rsqrt(var + LN_EPS) * g + b


def _assemble4(cols, dtype):
    tm = cols[0].shape[0]
    lane = lax.broadcasted_iota(i32, (tm, TOP_K), 1)
    out = jnp.broadcast_to(cols[TOP_K - 1], (tm, TOP_K))
    for k in range(TOP_K - 2, -1, -1):
        out = jnp.where(lane == k, jnp.broadcast_to(cols[k], (tm, TOP_K)), out)
    return out.astype(dtype)


def _merge_kernel(oa_ref, ob_ref, ga_ref, gb_ref, x_ref, wa_ref, wb_ref, wo_ref, g1_ref, b1_ref, rw_ref, rb_ref,
                  tri_ref, h1_ref, idx_ref, tw_ref, rank_ref, cnt_ref, carry_ref):
    @pl.when(pl.program_id(0) == 0)
    def _():
        carry_ref[...] = jnp.zeros_like(carry_ref)

    a = _dot(oa_ref[...], wa_ref[...])
    b = _dot(ob_ref[...], wb_ref[...])
    merged = _sigmoid(ga_ref[...].astype(f32)) * a + _sigmoid(gb_ref[...].astype(f32)) * b
    mix = _dot(merged.astype(bf16), wo_ref[...])
    h1 = _layer_norm(DN_ALPHA * x_ref[...] + mix, g1_ref[...], b1_ref[...])
    h1_ref[...] = h1

    logits = _dot(h1.astype(bf16), rw_ref[...]) + rb_ref[...]
    tm = logits.shape[0]
    lane = lax.broadcasted_iota(i32, (tm, N_EXPERTS), 1).astype(f32)
    work = logits
    vals, idxs = [], []
    for _ in range(TOP_K):
        mk = jnp.max(work, axis=1, keepdims=True)
        ik = jnp.min(jnp.where(work == mk, lane, float(N_EXPERTS)), axis=1, keepdims=True)
        vals.append(mk)
        idxs.append(ik)
        work = jnp.where(lane == ik, -jnp.inf, work)
    es = [jnp.exp(v - vals[0]) for v in vals]
    den = es[0] + es[1] + es[2] + es[3]
    tw_ref[...] = _assemble4([e / den for e in es], f32)
    idx_ref[...] = _assemble4(idxs, i32)

    onehot = jnp.zeros((tm, N_EXPERTS), f32)
    for ik in idxs:
        onehot = onehot + jnp.where(lane == ik, 1.0, 0.0)
    before = _dot(tri_ref[...], onehot.astype(bf16)) + carry_ref[...]
    ranks = [jnp.sum(jnp.where(lane == ik, before, 0.0), axis=1, keepdims=True) for ik in idxs]
    rank_ref[...] = _assemble4(ranks, i32)
    total = carry_ref[...] + jnp.sum(onehot, axis=0, keepdims=True)
    carry_ref[...] = total
    cnt_ref[...] = total


def _merge(o_a, o_b, pb, x2d, wa, wb, wo, g1, b1, rw, rb):
    n = x2d.shape[0]
    tm = min(MERGE_TM, n)
    t = np.arange(tm)
    tri = jnp.asarray((t[None, :] < t[:, None]).astype(np.float32), dtype=bf16)
    row = lambda i: (i, 0)
    const = lambda i: (0, 0)
    return pl.pallas_call(
        _merge_kernel,
        grid=(n // tm,),
        in_specs=[
            pl.BlockSpec((tm, HG_WIDTH), row),
            pl.BlockSpec((tm, DA_WIDTH), row),
            pl.BlockSpec((tm, D_MODEL), lambda i: (i, 0)),
            pl.BlockSpec((tm, D_MODEL), lambda i: (i, 1)),
            pl.BlockSpec((tm, D_MODEL), row),
            pl.BlockSpec(wa.shape, const), pl.BlockSpec(wb.shape, const), pl.BlockSpec(wo.shape, const),
            pl.BlockSpec(g1.shape, const), pl.BlockSpec(b1.shape, const),
            pl.BlockSpec(rw.shape, const), pl.BlockSpec(rb.shape, const),
            pl.BlockSpec((tm, tm), const),
        ],
        out_specs=[
            pl.BlockSpec((tm, D_MODEL), row),
            pl.BlockSpec((tm, TOP_K), row), pl.BlockSpec((tm, TOP_K), row), pl.BlockSpec((tm, TOP_K), row),
            pl.BlockSpec((1, N_EXPERTS), const),
        ],
        out_shape=[
            jax.ShapeDtypeStruct((n, D_MODEL), f32),
            jax.ShapeDtypeStruct((n, TOP_K), i32), jax.ShapeDtypeStruct((n, TOP_K), f32),
            jax.ShapeDtypeStruct((n, TOP_K), i32),
            jax.ShapeDtypeStruct((1, N_EXPERTS), f32),
        ],
        scratch_shapes=[pltpu.VMEM((1, N_EXPERTS), f32)],
        compiler_params=pltpu.CompilerParams(
            dimension_semantics=("arbitrary",), vmem_limit_bytes=VMEM_LIMIT_BYTES),
        name="merge_ln_router",
    )(o_a, o_b, pb, pb, x2d, wa, wb, wo, g1, b1, rw, rb, tri)


def _row_copy_loop(n_rows, start_one):
    def body(t, carry):
        for k in range(TOP_K):
            start_one(t, k)
        return carry
    lax.fori_loop(0, n_rows, body, 0)


def _dispatch_kernel(zrow_ref, zflag_ref, pos_ref, h1_ref, xs_ref, zeros_ref, sem, zsem):
    tm = h1_ref.shape[0]

    @pl.when(pl.program_id(0) == 0)
    def _():
        zeros_ref[...] = jnp.zeros_like(zeros_ref)
        for e in range(2 * N_EXPERTS):
            @pl.when(zflag_ref[e] == 1)
            def _():
                z0 = pl.multiple_of(zrow_ref[e], MOE_T)
                cp = pltpu.make_async_copy(zeros_ref, xs_ref.at[pl.ds(z0, MOE_T)], zsem)
                cp.start()
                cp.wait()

    def row_copy(t, k):
        return pltpu.make_async_copy(h1_ref.at[pl.ds(t, 1)], xs_ref.at[pl.ds(pos_ref[t * TOP_K + k], 1)], sem)

    _row_copy_loop(tm, lambda t, k: row_copy(t, k).start(priority=k % 2))
    _row_copy_loop(tm, lambda t, k: row_copy(t, k).wait())


def _dispatch(h1, pos_flat, zrow, zflag, n_rows_sorted):
    n = h1.shape[0]
    tm = min(ROW_TM, n)
    return pl.pallas_call(
        _dispatch_kernel,
        grid_spec=pltpu.PrefetchScalarGridSpec(
            num_scalar_prefetch=2,
            grid=(n // tm,),
            in_specs=[
                pl.BlockSpec((tm * TOP_K,), lambda i, zr, zf: (i,), memory_space=pltpu.SMEM),
                pl.BlockSpec((tm, D_MODEL), lambda i, zr, zf: (i, 0)),
            ],
            out_specs=pl.BlockSpec(memory_space=pl.ANY),
            scratch_shapes=[pltpu.VMEM((MOE_T, D_MODEL), f32), pltpu.SemaphoreType.DMA, pltpu.SemaphoreType.DMA],
        ),
        out_shape=jax.ShapeDtypeStruct((n_rows_sorted, D_MODEL), f32),
        compiler_params=pltpu.CompilerParams(
            dimension_semantics=("arbitrary",), vmem_limit_bytes=VMEM_LIMIT_BYTES),
        name="moe_dispatch",
    )(zrow, zflag, pos_flat, h1)


def _ffn_kernel(te_ref, nt_ref, x_ref, wgu_ref, wd_ref, bgu_ref, bd_ref, y_ref):
    @pl.when(pl.program_id(0) < nt_ref[0])
    def _():
        x = x_ref[...].astype(bf16)
        acc = jnp.broadcast_to(bd_ref[0], y_ref.shape)
        even = (lax.broadcasted_iota(i32, (x.shape[0], LANES), 1) & 1) == 0
        for c0 in range(0, D_FF, FFN_FC):
            cols = slice(2 * c0, 2 * (c0 + FFN_FC))
            hgu = _dot(x, wgu_ref[0, :, cols]) + bgu_ref[0, :, cols]
            zs = []
            for j in range(0, 2 * FFN_FC, 2 * LANES):
                lo = hgu[:, j:j + LANES]
                hi = hgu[:, j + LANES:j + 2 * LANES]
                gate = jnp.where(even, lo, pltpu.roll(hi, 1, axis=1))
                up = jnp.where(even, pltpu.roll(lo, LANES - 1, axis=1), hi)
                gate = jnp.minimum(gate, SWIGLU_LIMIT)
                up = jnp.clip(up, -SWIGLU_LIMIT, SWIGLU_LIMIT)
                glu = gate * _sigmoid(gate * SWIGLU_ALPHA)
                zs.append(((up + 1.0) * glu).astype(bf16))
            acc = acc + _dot(jnp.concatenate(zs, axis=1), wd_ref[0, c0:c0 + FFN_FC, :])
        y_ref[...] = acc

    @pl.when(pl.program_id(0) >= nt_ref[0])
    def _():
        y_ref[...] = jnp.zeros_like(y_ref)


def _ffn(xs, tile_e, n_tiles, wgu, wd, bgu, bd):
    p = xs.shape[0]
    last = lambda j, te, nt: jnp.minimum(j, nt[0] - 1)
    wmap = lambda j, te, nt: (te[j], 0, 0)
    return pl.pallas_call(
        _ffn_kernel,
        grid_spec=pltpu.PrefetchScalarGridSpec(
            num_scalar_prefetch=2,
            grid=(p // MOE_T,),
            in_specs=[
                pl.BlockSpec((MOE_T, D_MODEL), lambda j, te, nt: (last(j, te, nt), 0)),
                pl.BlockSpec((1, D_MODEL, 2 * D_FF), wmap),
                pl.BlockSpec((1, D_FF, D_MODEL), wmap),
                pl.BlockSpec((1, 1, 2 * D_FF), wmap),
                pl.BlockSpec((1, 1, D_MODEL), wmap),
            ],
            out_specs=pl.BlockSpec((MOE_T, D_MODEL), lambda j, te, nt: (j, 0)),
        ),
        out_shape=jax.ShapeDtypeStruct((p, D_MODEL), f32),
        compiler_params=pltpu.CompilerParams(
            dimension_semantics=("arbitrary",), vmem_limit_bytes=VMEM_LIMIT_BYTES),
        name="moe_ffn",
    )(tile_e, n_tiles, xs, wgu, wd, bgu, bd)


def _combine_kernel(pos_ref, h1_ref, tw_ref, g2_ref, b2_ref, ys_ref, o_ref, ybuf_ref, sem):
    tm = h1_ref.shape[0]

    def row_copy(t, k):
        return pltpu.make_async_copy(ys_ref.at[pl.ds(pos_ref[t * TOP_K + k], 1)], ybuf_ref.at[k, pl.ds(t, 1)], sem)

    _row_copy_loop(tm, lambda t, k: row_copy(t, k).start(priority=k % 2))
    _row_copy_loop(tm, lambda t, k: row_copy(t, k).wait())

    tw = tw_ref[...]
    ffn = tw[:, 0:1] * ybuf_ref[0]
    for k in range(1, TOP_K):
        ffn = ffn + tw[:, k:k + 1] * ybuf_ref[k]
    o_ref[...] = _layer_norm(DN_ALPHA * h1_ref[...] + ffn, g2_ref[...], b2_ref[...])


def _combine(h1, tw, pos_flat, ys, g2, b2):
    n = h1.shape[0]
    tm = min(ROW_TM, n)
    return pl.pallas_call(
        _combine_kernel,
        grid=(n // tm,),
        in_specs=[
            pl.BlockSpec((tm * TOP_K,), lambda i: (i,), memory_space=pltpu.SMEM),
            pl.BlockSpec((tm, D_MODEL), lambda i: (i, 0)),
            pl.BlockSpec((tm, TOP_K), lambda i: (i, 0)),
            pl.BlockSpec(g2.shape, lambda i: (0, 0)), pl.BlockSpec(b2.shape, lambda i: (0, 0)),
            pl.BlockSpec(memory_space=pl.ANY),
        ],
        out_specs=pl.BlockSpec((tm, D_MODEL), lambda i: (i, 0)),
        out_shape=jax.ShapeDtypeStruct((n, D_MODEL), f32),
        scratch_shapes=[pltpu.VMEM((TOP_K, tm, D_MODEL), f32), pltpu.SemaphoreType.DMA],
        compiler_params=pltpu.CompilerParams(
            dimension_semantics=("arbitrary",), vmem_limit_bytes=VMEM_LIMIT_BYTES),
        name="moe_combine_ln",
    )(pos_flat, h1, tw, g2, b2, ys)


def _routing_tables(counts, idx, rank, n_tiles_max):
    padded = ((counts + MOE_T - 1) // MOE_T) * MOE_T
    ends = jnp.cumsum(padded)
    offs = ends - padded
    pos = (offs[idx] + rank).reshape(-1)
    n_tiles = (ends[-1] // MOE_T).astype(i32)
    tile_start = jnp.minimum(jnp.arange(n_tiles_max, dtype=i32), n_tiles - 1) * MOE_T
    tile_e = jnp.minimum(jnp.sum((ends[None, :] <= tile_start[:, None]).astype(i32), axis=1), N_EXPERTS - 1)
    trailing = jnp.arange(n_tiles_max - N_EXPERTS, n_tiles_max, dtype=i32)
    zrow = jnp.concatenate([(ends - MOE_T).astype(i32), trailing * MOE_T])
    zflag = jnp.concatenate([(padded > counts).astype(i32), (trailing >= n_tiles).astype(i32)])
    return pos.astype(i32), tile_e, n_tiles.reshape(1), zrow, zflag


def kernel(x, w_in, hg_lb_logits, hg_norm_g, da_lambda, da_norm_g, w_branch_a, w_branch_b, w_out, ln1_g, ln1_b,
           router_w, router_b, w_gate_up, b_gate_up, w_down, b_down, ln2_g, ln2_b):
    batch, seq, d = x.shape
    assert d == D_MODEL and w_in.shape[0] == DEPTH == 1
    n = batch * seq
    x2d = x.reshape(n, d)

    w = w_in[0]
    w_f = w[:, 0:2 * HG_KEY].astype(bf16)
    o_hi, o_dq, o_ga = 2 * HG_KEY, 2 * HG_KEY + 2 * HG_WIDTH, 2 * HG_KEY + 2 * HG_WIDTH + 2 * DA_QK + DA_WIDTH
    w_b = jnp.concatenate([w[:, o_ga:], w[:, o_hi:o_ga]], axis=1).astype(bf16)
    pf = _inproj(x2d, w_f, f32)
    pb = _inproj(x2d, w_b, bf16)

    o_a = _hgrn(pf, pb, hg_lb_logits, hg_norm_g[0].reshape(1, HG_WIDTH), batch, seq)
    o_b = _attn(pb, da_lambda[0], da_norm_g[0], batch, seq)

    h1, idx, tw, rank, cnt = _merge(
        o_a, o_b, pb, x2d, w_branch_a[0].astype(bf16), w_branch_b[0].astype(bf16), w_out[0].astype(bf16),
        ln1_g[0].reshape(1, d), ln1_b[0].reshape(1, d), router_w[0].astype(bf16), router_b[0].reshape(1, N_EXPERTS))

    n_rows_sorted = n * TOP_K + N_EXPERTS * MOE_T
    pos, tile_e, n_tiles, zrow, zflag = _routing_tables(cnt[0].astype(i32), idx, rank, n_rows_sorted // MOE_T)

    xs = _dispatch(h1, pos, zrow, zflag, n_rows_sorted)
    wd = w_down[0].reshape(N_EXPERTS, D_FF // LANES, 2, LANES // 2, d).transpose(0, 1, 3, 2, 4)
    ys = _ffn(xs, tile_e, n_tiles, w_gate_up[0].astype(bf16), wd.reshape(N_EXPERTS, D_FF, d).astype(bf16),
              b_gate_up[0][:, None, :], b_down[0][:, None, :])
    out = _combine(h1, tw, pos, ys, ln2_g[0].reshape(1, d), ln2_b[0].reshape(1, d))
    return out.reshape(batch, seq, d)
```

```python
import functools
import math

import jax
import jax.numpy as jnp
import numpy as np
from jax import lax
from jax.experimental import pallas as pl
from jax.experimental.pallas import tpu as pltpu

f32 = jnp.float32
bf16 = jnp.bfloat16
i32 = jnp.int32

D_MODEL = 1024
DEPTH = 1
HG_HEADS = 4
HG_DK = 128
HG_DV = 128
HG_KEY = HG_HEADS * HG_DK
HG_WIDTH = HG_HEADS * HG_DV
DA_HEADS = 4
DA_HD = 64
DA_DV = 2 * DA_HD
DA_QK = DA_HEADS * 2 * DA_HD
DA_WIDTH = DA_HEADS * DA_DV
N_EXPERTS = 32
TOP_K = 4
D_FF = 1024
SWIGLU_LIMIT = 7.0
SWIGLU_ALPHA = 1.702
DN_ALPHA = (2.0 * DEPTH) ** 0.25
LN_EPS = 1e-5
NORM_EPS = 1e-6
LAMBDA_INIT = 0.8 - 0.6 * math.exp(-0.3 * 0)

VMEM_LIMIT_BYTES = 52 * 1024 * 1024

PROJ_TM = 1024
PROJ_TN = 512
HG_CHUNK = 128
HG_ROWS = 512
ATT_T = 256
ATT_R = 128
MERGE_TM = 512
MOE_T = 256
ROW_TM = 256
FFN_FC = 512
LANES = 128


def _sigmoid(x):
    return 0.5 * jnp.tanh(0.5 * x) + 0.5


def _dot(a, b):
    return jnp.dot(a, b, preferred_element_type=f32)


def _dot_nt(a, b):
    return lax.dot_general(a, b, (((1,), (1,)), ((), ())), preferred_element_type=f32)


def _dot_tn(a, b):
    return lax.dot_general(a, b, (((0,), (0,)), ((), ())), preferred_element_type=f32)


def _inproj_kernel(x_ref, w_ref, o_ref, xb_ref):
    @pl.when(pl.program_id(1) == 0)
    def _():
        xb_ref[...] = x_ref[...].astype(bf16)

    o_ref[...] = _dot(xb_ref[...], w_ref[...]).astype(o_ref.dtype)


def _inproj(x2d, w, out_dtype):
    n, k = x2d.shape
    m = w.shape[1]
    tm = min(PROJ_TM, n)
    return pl.pallas_call(
        _inproj_kernel,
        grid=(n // tm, m // PROJ_TN),
        in_specs=[pl.BlockSpec((tm, k), lambda i, j: (i, 0)),
                  pl.BlockSpec((k, PROJ_TN), lambda i, j: (0, j))],
        out_specs=pl.BlockSpec((tm, PROJ_TN), lambda i, j: (i, j)),
        out_shape=jax.ShapeDtypeStruct((n, m), out_dtype),
        scratch_shapes=[pltpu.VMEM((tm, k), bf16)],
        compiler_params=pltpu.CompilerParams(
            dimension_semantics=("parallel", "arbitrary"), vmem_limit_bytes=VMEM_LIMIT_BYTES),
        name="inproj",
    )(x2d, w)


def _shift_down(x, d):
    n = x.shape[0]
    if d % 8 == 0:
        return jnp.concatenate([x[n - d:], x[:n - d]], axis=0)
    return pltpu.roll(x, d, axis=0)


def _shift_up(x, d):
    n = x.shape[0]
    if d % 8 == 0:
        return jnp.concatenate([x[d:], x[:d]], axis=0)
    return pltpu.roll(x, n - d, axis=0)


def _hgrn_chunk_head(qv, fl, v, g, lb, ng, state, tril, lvl, row, eye):
    c = qv.shape[0]
    f = lb + (1.0 - lb) * _sigmoid(fl)
    logf = jnp.log(f)
    kk = 1.0 - f
    qa = qv * _sigmoid(qv)

    l1 = logf.astype(bf16)
    r1 = logf - l1.astype(f32)
    l2 = r1.astype(bf16)
    l3 = (r1 - l2.astype(f32)).astype(bf16)
    bc = _dot(tril, jnp.concatenate([l1, l2, l3], axis=1))
    b = bc[:, 0:HG_DK] + bc[:, HG_DK:2 * HG_DK] + bc[:, 2 * HG_DK:3 * HG_DK]

    attn = jnp.where(lvl == -1, _dot_nt(qa.astype(bf16), kk.astype(bf16)), 0.0)
    filled = b
    d = 1
    level = 0
    while d < c:
        upper = (row & d) != 0
        ref_b = jnp.where(upper, _shift_down(filled, d), filled)
        diff = b - ref_b
        e = jnp.exp(jnp.where(upper, diff, -diff))
        mixed = (jnp.where(upper, qa, kk) * e).astype(bf16)
        attn = jnp.where(lvl == level, _dot_nt(mixed, mixed), attn)
        filled = jnp.where(upper, filled, _shift_up(filled, d))
        d *= 2
        level += 1

    o_intra = _dot(attn.astype(bf16), v)
    o_inter = _dot((qa * jnp.exp(b)).astype(bf16), state.astype(bf16))

    b_last = b[c - 1:c, :]
    k_dec = (kk * jnp.exp(b_last - b)).astype(bf16)
    dcol = jnp.sum(jnp.where(eye, jnp.broadcast_to(jnp.exp(b_last), eye.shape), 0.0), axis=1, keepdims=True)
    new_state = dcol * state + _dot_tn(k_dec, v)

    o = o_inter + o_intra
    ms = jnp.mean(o * o, axis=1, keepdims=True)
    o = o * lax.rsqrt(ms + NORM_EPS) * ng
    o = o * _sigmoid(g.astype(f32))
    return o.astype(bf16), new_state


def _hgrn_kernel(q_ref, f_ref, i_ref, g_ref, lbl_ref, ng_ref, tril_ref, lvl_ref, o_ref, st_ref, *, chunk, n_chunks):
    @pl.when(pl.program_id(1) == 0)
    def _():
        st_ref[...] = jnp.zeros_like(st_ref)

    lbl = lbl_ref[...]
    ex = jnp.exp(lbl - jnp.max(lbl, axis=0, keepdims=True))
    lb_all = ex[0:1, :] / jnp.sum(ex, axis=0, keepdims=True)
    ng_all = ng_ref[...]
    tril = tril_ref[...]
    lvl = lvl_ref[...]
    row = lax.broadcasted_iota(i32, (chunk, HG_DK), 0)
    eye = lax.broadcasted_iota(i32, (HG_DK, HG_DV), 0) == lax.broadcasted_iota(i32, (HG_DK, HG_DV), 1)

    def body(ci, carry):
        r0 = pl.multiple_of(ci * chunk, chunk)
        for h in range(HG_HEADS):
            cs = pl.ds(h * HG_DK, HG_DK)
            out, new_state = _hgrn_chunk_head(
                q_ref[pl.ds(r0, chunk), cs], f_ref[pl.ds(r0, chunk), cs],
                i_ref[pl.ds(r0, chunk), cs], g_ref[pl.ds(r0, chunk), cs],
                lb_all[:, h * HG_DK:(h + 1) * HG_DK], ng_all[:, h * HG_DV:(h + 1) * HG_DV],
                st_ref[h], tril, lvl, row, eye)
            o_ref[pl.ds(r0, chunk), cs] = out
            st_ref[h] = new_state
        return carry

    lax.fori_loop(0, n_chunks, body, 0)


def _hgrn_consts(chunk):
    t = np.arange(chunk)
    tril = (t[None, :] <= t[:, None]).astype(np.float32)
    x = t[:, None] ^ t[None, :]
    lvl = np.where(x > 0, np.floor(np.log2(np.maximum(x, 1))).astype(np.int32), -1)
    lvl = np.where(t[:, None] >= t[None, :], lvl, -2).astype(np.int32)
    return jnp.asarray(tril, dtype=bf16), jnp.asarray(lvl, dtype=i32)


def _hgrn(pf, pb, lb_logits, norm_g, batch, seq):
    n = batch * seq
    rows = min(HG_ROWS, seq)
    chunk = min(HG_CHUNK, rows)
    spb = seq // rows
    tril, lvl = _hgrn_consts(chunk)
    row_blk = lambda b, s: b * spb + s
    return pl.pallas_call(
        functools.partial(_hgrn_kernel, chunk=chunk, n_chunks=rows // chunk),
        grid=(batch, spb),
        in_specs=[
            pl.BlockSpec((rows, HG_KEY), lambda b, s: (row_blk(b, s), 0)),
            pl.BlockSpec((rows, HG_KEY), lambda b, s: (row_blk(b, s), 1)),
            pl.BlockSpec((rows, HG_WIDTH), lambda b, s: (row_blk(b, s), 4)),
            pl.BlockSpec((rows, HG_WIDTH), lambda b, s: (row_blk(b, s), 5)),
            pl.BlockSpec(lb_logits.shape, lambda b, s: (0, 0)),
            pl.BlockSpec((1, HG_WIDTH), lambda b, s: (0, 0)),
            pl.BlockSpec((chunk, chunk), lambda b, s: (0, 0)),
            pl.BlockSpec((chunk, chunk), lambda b, s: (0, 0)),
        ],
        out_specs=pl.BlockSpec((rows, HG_WIDTH), lambda b, s: (row_blk(b, s), 0)),
        out_shape=jax.ShapeDtypeStruct((n, HG_WIDTH), bf16),
        scratch_shapes=[pltpu.VMEM((HG_HEADS, HG_DK, HG_DV), f32)],
        compiler_params=pltpu.CompilerParams(
            dimension_semantics=("parallel", "arbitrary"), vmem_limit_bytes=VMEM_LIMIT_BYTES),
        name="hgrn2",
    )(pf, pf, pb, pb, lb_logits, norm_g, tril, lvl)


def _attn_kernel(lam_ref, ng_ref, q_ref, k_ref, v_ref, o_ref, qs_ref, m_ref, acc_ref, *, t):
    qi = pl.program_id(1)
    blk = 2 * DA_HD

    lane = lax.broadcasted_iota(i32, (t, blk), 1)
    for h in range(DA_HEADS):
        q = q_ref[:, h * blk:(h + 1) * blk] * jnp.asarray(DA_HD ** -0.5, bf16)
        zero = jnp.zeros_like(q)
        qs_ref[h, 0:t, :] = jnp.where(lane < DA_HD, q, zero)
        qs_ref[h, t:2 * t, :] = jnp.where(lane >= DA_HD, q, zero)
    m_ref[...] = jnp.full_like(m_ref, -jnp.inf)
    acc_ref[...] = jnp.zeros_like(acc_ref)

    col = lax.broadcasted_iota(i32, (1, t), 1)
    ones = jnp.ones((t, DA_DV), bf16)

    def step(kt, masked):
        k0 = pl.multiple_of(kt * t, t)
        rel = ((kt - qi) * t + col).astype(f32)
        if masked:
            rr = lax.broadcasted_iota(i32, (2 * t, t), 0)
            rr = jnp.where(rr >= t, rr - t, rr)
            causal = lax.broadcasted_iota(i32, (2 * t, t), 1) <= rr
        for h in range(DA_HEADS):
            cs = pl.ds(h * blk, blk)
            k = k_ref[pl.ds(k0, t), cs]
            vo = jnp.concatenate([v_ref[pl.ds(k0, t), cs], ones], axis=1)
            bias = (2.0 ** (-8.0 * (h + 1) / DA_HEADS)) * rel
            for r0 in range(0, 2 * t, ATT_R):
                rows = slice(r0, r0 + ATT_R)
                s = _dot_nt(qs_ref[h, rows, :], k) + bias
                if masked:
                    s = jnp.where(causal[rows, :], s, -jnp.inf)
                m_old = m_ref[h, rows, :]
                m_new = jnp.maximum(m_old, jnp.max(s, axis=1, keepdims=True))
                alpha = jnp.exp(m_old - m_new)
                p = jnp.exp(s - jnp.concatenate([m_new] * (t // DA_DV), axis=1)).astype(bf16)
                acc_ref[h, rows, :] = jnp.concatenate([alpha, alpha], axis=1) * acc_ref[h, rows, :] + _dot(p, vo)
                m_ref[h, rows, :] = m_new

    def loop_body(kt, carry):
        step(kt, False)
        return carry

    lax.fori_loop(0, qi, loop_body, 0)
    step(qi, True)

    lp = lam_ref[...]
    lam = (jnp.exp(jnp.sum(lp[0:1, :] * lp[1:2, :], axis=1, keepdims=True))
           - jnp.exp(jnp.sum(lp[2:3, :] * lp[3:4, :], axis=1, keepdims=True)) + LAMBDA_INIT)
    for h in range(DA_HEADS):
        acc = acc_ref[h]
        o_all = acc[:, 0:DA_DV] / acc[:, DA_DV:2 * DA_DV]
        o = o_all[0:t, :] - lam * o_all[t:2 * t, :]
        ms = jnp.mean(o * o, axis=1, keepdims=True)
        o = o * lax.rsqrt(ms + NORM_EPS) * ng_ref[:, h * DA_DV:(h + 1) * DA_DV] * (1.0 - LAMBDA_INIT)
        o_ref[:, h * DA_DV:(h + 1) * DA_DV] = o.astype(bf16)


def _attn(pb, lam_params, norm_g, batch, seq):
    n = batch * seq
    t = min(ATT_T, seq)
    nq = seq // t
    q0, k0, v0 = 3072 // DA_QK, 3584 // DA_QK, 4096 // DA_WIDTH
    return pl.pallas_call(
        functools.partial(_attn_kernel, t=t),
        grid=(batch, nq),
        in_specs=[
            pl.BlockSpec(lam_params.shape, lambda b, i: (0, 0)),
            pl.BlockSpec((1, DA_WIDTH), lambda b, i: (0, 0)),
            pl.BlockSpec((t, DA_QK), lambda b, i: (b * nq + i, q0)),
            pl.BlockSpec((seq, DA_QK), lambda b, i: (b, k0)),
            pl.BlockSpec((seq, DA_WIDTH), lambda b, i: (b, v0)),
        ],
        out_specs=pl.BlockSpec((t, DA_WIDTH), lambda b, i: (b * nq + i, 0)),
        out_shape=jax.ShapeDtypeStruct((n, DA_WIDTH), bf16),
        scratch_shapes=[pltpu.VMEM((DA_HEADS, 2 * t, 2 * DA_HD), bf16), pltpu.VMEM((DA_HEADS, 2 * t, DA_DV), f32),
                        pltpu.VMEM((DA_HEADS, 2 * t, 2 * DA_DV), f32)],
        compiler_params=pltpu.CompilerParams(
            dimension_semantics=("parallel", "arbitrary"), vmem_limit_bytes=VMEM_LIMIT_BYTES),
        name="diff_attn",
    )(lam_params, norm_g.reshape(1, DA_WIDTH), pb, pb, pb)


def _layer_norm(y, g, b):
    mu = jnp.mean(y, axis=1, keepdims=True)
    yc = y - mu
    var = jnp.mean(yc * yc, axis=1, keepdims=True)
    return yc * lax.rsqrt(var + LN_EPS) * g + b


def _assemble4(cols, dtype):
    tm = cols[0].shape[0]
    lane = lax.broadcasted_iota(i32, (tm, TOP_K), 1)
    out = jnp.broadcast_to(cols[TOP_K - 1], (tm, TOP_K))
    for k in range(TOP_K - 2, -1, -1):
        out = jnp.where(lane == k, jnp.broadcast_to(cols[k], (tm, TOP_K)), out)
    return out.astype(dtype)


def _merge_kernel(oa_ref, ob_ref, ga_ref, gb_ref, x_ref, wa_ref, wb_ref, wo_ref, g1_ref, b1_ref, rw_ref, rb_ref,
                  tri_ref, h1_ref, idx_ref, tw_ref, rank_ref, cnt_ref, carry_ref):
    @pl.when(pl.program_id(0) == 0)
    def _():
        carry_ref[...] = jnp.zeros_like(carry_ref)

    a = _dot(oa_ref[...], wa_ref[...])
    b = _dot(ob_ref[...], wb_ref[...])
    merged = _sigmoid(ga_ref[...].astype(f32)) * a + _sigmoid(gb_ref[...].astype(f32)) * b
    mix = _dot(merged.astype(bf16), wo_ref[...])
    h1 = _layer_norm(DN_ALPHA * x_ref[...] + mix, g1_ref[...], b1_ref[...])
    h1_ref[...] = h1

    logits = _dot(h1.astype(bf16), rw_ref[...]) + rb_ref[...]
    tm = logits.shape[0]
    lane = lax.broadcasted_iota(i32, (tm, N_EXPERTS), 1).astype(f32)
    work = logits
    vals, idxs = [], []
    for _ in range(TOP_K):
        mk = jnp.max(work, axis=1, keepdims=True)
        ik = jnp.min(jnp.where(work == mk, lane, float(N_EXPERTS)), axis=1, keepdims=True)
        vals.append(mk)
        idxs.append(ik)
        work = jnp.where(lane == ik, -jnp.inf, work)
    es = [jnp.exp(v - vals[0]) for v in vals]
    den = es[0] + es[1] + es[2] + es[3]
    tw_ref[...] = _assemble4([e / den for e in es], f32)
    idx_ref[...] = _assemble4(idxs, i32)

    onehot = jnp.zeros((tm, N_EXPERTS), f32)
    for ik in idxs:
        onehot = onehot + jnp.where(lane == ik, 1.0, 0.0)
    before = _dot(tri_ref[...], onehot.astype(bf16)) + carry_ref[...]
    ranks = [jnp.sum(jnp.where(lane == ik, before, 0.0), axis=1, keepdims=True) for ik in idxs]
    rank_ref[...] = _assemble4(ranks, i32)
    total = carry_ref[...] + jnp.sum(onehot, axis=0, keepdims=True)
    carry_ref[...] = total
    cnt_ref[...] = total


def _merge(o_a, o_b, pb, x2d, wa, wb, wo, g1, b1, rw, rb):
    n = x2d.shape[0]
    tm = min(MERGE_TM, n)
    t = np.arange(tm)
    tri = jnp.asarray((t[None, :] < t[:, None]).astype(np.float32), dtype=bf16)
    row = lambda i: (i, 0)
    const = lambda i: (0, 0)
    return pl.pallas_call(
        _merge_kernel,
        grid=(n // tm,),
        in_specs=[
            pl.BlockSpec((tm, HG_WIDTH), row),
            pl.BlockSpec((tm, DA_WIDTH), row),
            pl.BlockSpec((tm, D_MODEL), lambda i: (i, 0)),
            pl.BlockSpec((tm, D_MODEL), lambda i: (i, 1)),
            pl.BlockSpec((tm, D_MODEL), row),
            pl.BlockSpec(wa.shape, const), pl.BlockSpec(wb.shape, const), pl.BlockSpec(wo.shape, const),
            pl.BlockSpec(g1.shape, const), pl.BlockSpec(b1.shape, const),
            pl.BlockSpec(rw.shape, const), pl.BlockSpec(rb.shape, const),
            pl.BlockSpec((tm, tm), const),
        ],
        out_specs=[
            pl.BlockSpec((tm, D_MODEL), row),
            pl.BlockSpec((tm, TOP_K), row), pl.BlockSpec((tm, TOP_K), row), pl.BlockSpec((tm, TOP_K), row),
            pl.BlockSpec((1, N_EXPERTS), const),
        ],
        out_shape=[
            jax.ShapeDtypeStruct((n, D_MODEL), f32),
            jax.ShapeDtypeStruct((n, TOP_K), i32), jax.ShapeDtypeStruct((n, TOP_K), f32),
            jax.ShapeDtypeStruct((n, TOP_K), i32),
            jax.ShapeDtypeStruct((1, N_EXPERTS), f32),
        ],
        scratch_shapes=[pltpu.VMEM((1, N_EXPERTS), f32)],
        compiler_params=pltpu.CompilerParams(
            dimension_semantics=("arbitrary",), vmem_limit_bytes=VMEM_LIMIT_BYTES),
        name="merge_ln_router",
    )(o_a, o_b, pb, pb, x2d, wa, wb, wo, g1, b1, rw, rb, tri)


def _row_copy_loop(n_rows, start_one):
    def body(t, carry):
        for k in range(TOP_K):
            start_one(t, k)
        return carry
    lax.fori_loop(0, n_rows, body, 0)


def _dispatch_kernel(zrow_ref, zflag_ref, pos_ref, h1_ref, xs_ref, zeros_ref, sem, zsem):
    tm = h1_ref.shape[0]

    @pl.when(pl.program_id(0) == 0)
    def _():
        zeros_ref[...] = jnp.zeros_like(zeros_ref)
        for e in range(2 * N_EXPERTS):
            @pl.when(zflag_ref[e] == 1)
            def _():
                z0 = pl.multiple_of(zrow_ref[e], MOE_T)
                cp = pltpu.make_async_copy(zeros_ref, xs_ref.at[pl.ds(z0, MOE_T)], zsem)
                cp.start()
                cp.wait()

    def row_copy(t, k):
        return pltpu.make_async_copy(h1_ref.at[pl.ds(t, 1)], xs_ref.at[pl.ds(pos_ref[t * TOP_K + k], 1)], sem)

    _row_copy_loop(tm, lambda t, k: row_copy(t, k).start(priority=k % 2))
    _row_copy_loop(tm, lambda t, k: row_copy(t, k).wait())


def _dispatch(h1, pos_flat, zrow, zflag, n_rows_sorted):
    n = h1.shape[0]
    tm = min(ROW_TM, n)
    return pl.pallas_call(
        _dispatch_kernel,
        grid_spec=pltpu.PrefetchScalarGridSpec(
            num_scalar_prefetch=2,
            grid=(n // tm,),
            in_specs=[
                pl.BlockSpec((tm * TOP_K,), lambda i, zr, zf: (i,), memory_space=pltpu.SMEM),
                pl.BlockSpec((tm, D_MODEL), lambda i, zr, zf: (i, 0)),
            ],
            out_specs=pl.BlockSpec(memory_space=pl.ANY),
            scratch_shapes=[pltpu.VMEM((MOE_T, D_MODEL), f32), pltpu.SemaphoreType.DMA, pltpu.SemaphoreType.DMA],
        ),
        out_shape=jax.ShapeDtypeStruct((n_rows_sorted, D_MODEL), f32),
        compiler_params=pltpu.CompilerParams(
            dimension_semantics=("arbitrary",), vmem_limit_bytes=VMEM_LIMIT_BYTES),
        name="moe_dispatch",
    )(zrow, zflag, pos_flat, h1)


def _ffn_kernel(te_ref, nt_ref, x_ref, wgu_ref, wd_ref, bgu_ref, bd_ref, perm_ref, y_ref, wgu_s, wd_s):
    j = pl.program_id(0)

    @pl.when(j < nt_ref[0])
    def _():
        @pl.when(jnp.logical_or(j == 0, te_ref[j] != te_ref[jnp.maximum(j - 1, 0)]))
        def _():
            for r0 in range(0, D_MODEL, LANES):
                wgu_s[r0:r0 + LANES, :] = wgu_ref[0, r0:r0 + LANES, :].astype(bf16)
            for r0 in range(0, D_FF, LANES):
                wd_s[r0:r0 + LANES, :] = _dot(perm_ref[...], wd_ref[0, r0:r0 + LANES, :].astype(bf16)).astype(bf16)

        x = x_ref[...].astype(bf16)
        acc = jnp.broadcast_to(bd_ref[0], y_ref.shape)
        even = (lax.broadcasted_iota(i32, (x.shape[0], LANES), 1) & 1) == 0
        for c0 in range(0, D_FF, FFN_FC):
            cols = slice(2 * c0, 2 * (c0 + FFN_FC))
            hgu = _dot(x, wgu_s[:, cols]) + bgu_ref[0, :, cols]
            zs = []
            for j0 in range(0, 2 * FFN_FC, 2 * LANES):
                lo = hgu[:, j0:j0 + LANES]
                hi = hgu[:, j0 + LANES:j0 + 2 * LANES]
                gate = jnp.where(even, lo, pltpu.roll(hi, 1, axis=1))
                up = jnp.where(even, pltpu.roll(lo, LANES - 1, axis=1), hi)
                gate = jnp.minimum(gate, SWIGLU_LIMIT)
                up = jnp.clip(up, -SWIGLU_LIMIT, SWIGLU_LIMIT)
                glu = gate * _sigmoid(gate * SWIGLU_ALPHA)
                zs.append(((up + 1.0) * glu).astype(bf16))
            acc = acc + _dot(jnp.concatenate(zs, axis=1), wd_s[c0:c0 + FFN_FC, :])
        y_ref[...] = acc

    @pl.when(j >= nt_ref[0])
    def _():
        y_ref[...] = jnp.zeros_like(y_ref)


def _ffn(xs, tile_e, n_tiles, wgu, wd, bgu, bd):
    p = xs.shape[0]
    last = lambda j, te, nt: jnp.minimum(j, nt[0] - 1)
    wmap = lambda j, te, nt: (0, te[j], 0, 0)
    bmap = lambda j, te, nt: (te[j], 0, 0)
    r = np.arange(LANES)
    perm = np.zeros((LANES, LANES), np.float32)
    perm[r, (r % 2) * (LANES // 2) + r // 2] = 1.0
    return pl.pallas_call(
        _ffn_kernel,
        grid_spec=pltpu.PrefetchScalarGridSpec(
            num_scalar_prefetch=2,
            grid=(p // MOE_T,),
            in_specs=[
                pl.BlockSpec((MOE_T, D_MODEL), lambda j, te, nt: (last(j, te, nt), 0)),
                pl.BlockSpec((None, 1, D_MODEL, 2 * D_FF), wmap),
                pl.BlockSpec((None, 1, D_FF, D_MODEL), wmap),
                pl.BlockSpec((1, 1, 2 * D_FF), bmap),
                pl.BlockSpec((1, 1, D_MODEL), bmap),
                pl.BlockSpec((LANES, LANES), lambda j, te, nt: (0, 0)),
            ],
            out_specs=pl.BlockSpec((MOE_T, D_MODEL), lambda j, te, nt: (j, 0)),
            scratch_shapes=[pltpu.VMEM((D_MODEL, 2 * D_FF), bf16), pltpu.VMEM((D_FF, D_MODEL), bf16)],
        ),
        out_shape=jax.ShapeDtypeStruct((p, D_MODEL), f32),
        compiler_params=pltpu.CompilerParams(
            dimension_semantics=("arbitrary",), vmem_limit_bytes=VMEM_LIMIT_BYTES),
        name="moe_ffn",
    )(tile_e, n_tiles, xs, wgu, wd, bgu, bd, jnp.asarray(perm, dtype=bf16))


def _combine_kernel(pos_ref, h1_ref, tw_ref, g2_ref, b2_ref, ys_ref, o_ref, ybuf_ref, sem):
    tm = h1_ref.shape[0]

    def row_copy(t, k):
        return pltpu.make_async_copy(ys_ref.at[pl.ds(pos_ref[t * TOP_K + k], 1)], ybuf_ref.at[k, pl.ds(t, 1)], sem)

    _row_copy_loop(tm, lambda t, k: row_copy(t, k).start(priority=k % 2))
    _row_copy_loop(tm, lambda t, k: row_copy(t, k).wait())

    tw = tw_ref[...]
    ffn = tw[:, 0:1] * ybuf_ref[0]
    for k in range(1, TOP_K):
        ffn = ffn + tw[:, k:k + 1] * ybuf_ref[k]
    o_ref[...] = _layer_norm(DN_ALPHA * h1_ref[...] + ffn, g2_ref[...], b2_ref[...])


def _combine(h1, tw, pos_flat, ys, g2, b2):
    n = h1.shape[0]
    tm = min(ROW_TM, n)
    return pl.pallas_call(
        _combine_kernel,
        grid=(n // tm,),
        in_specs=[
            pl.BlockSpec((tm * TOP_K,), lambda i: (i,), memory_space=pltpu.SMEM),
            pl.BlockSpec((tm, D_MODEL), lambda i: (i, 0)),
            pl.BlockSpec((tm, TOP_K), lambda i: (i, 0)),
            pl.BlockSpec(g2.shape, lambda i: (0, 0)), pl.BlockSpec(b2.shape, lambda i: (0, 0)),
            pl.BlockSpec(memory_space=pl.ANY),
        ],
        out_specs=pl.BlockSpec((tm, D_MODEL), lambda i: (i, 0)),
        out_shape=jax.ShapeDtypeStruct((n, D_MODEL), f32),
        scratch_shapes=[pltpu.VMEM((TOP_K, tm, D_MODEL), f32), pltpu.SemaphoreType.DMA],
        compiler_params=pltpu.CompilerParams(
            dimension_semantics=("arbitrary",), vmem_limit_bytes=VMEM_LIMIT_BYTES),
        name="moe_combine_ln",
    )(pos_flat, h1, tw, g2, b2, ys)


def _routing_tables(counts, idx, rank, n_tiles_max):
    padded = ((counts + MOE_T - 1) // MOE_T) * MOE_T
    ends = jnp.cumsum(padded)
    offs = ends - padded
    pos = (offs[idx] + rank).reshape(-1)
    n_tiles = (ends[-1] // MOE_T).astype(i32)
    tile_start = jnp.minimum(jnp.arange(n_tiles_max, dtype=i32), n_tiles - 1) * MOE_T
    tile_e = jnp.minimum(jnp.sum((ends[None, :] <= tile_start[:, None]).astype(i32), axis=1), N_EXPERTS - 1)
    trailing = jnp.arange(n_tiles_max - N_EXPERTS, n_tiles_max, dtype=i32)
    zrow = jnp.concatenate([(ends - MOE_T).astype(i32), trailing * MOE_T])
    zflag = jnp.concatenate([(padded > counts).astype(i32), (trailing >= n_tiles).astype(i32)])
    return pos.astype(i32), tile_e, n_tiles.reshape(1), zrow, zflag


def kernel(x, w_in, hg_lb_logits, hg_norm_g, da_lambda, da_norm_g, w_branch_a, w_branch_b, w_out, ln1_g, ln1_b,
           router_w, router_b, w_gate_up, b_gate_up, w_down, b_down, ln2_g, ln2_b):
    batch, seq, d = x.shape
    assert d == D_MODEL and w_in.shape[0] == DEPTH == 1
    n = batch * seq
    x2d = x.reshape(n, d)

    w = w_in[0]
    w_f = w[:, 0:2 * HG_KEY].astype(bf16)
    o_hi, o_dq, o_ga = 2 * HG_KEY, 2 * HG_KEY + 2 * HG_WIDTH, 2 * HG_KEY + 2 * HG_WIDTH + 2 * DA_QK + DA_WIDTH
    w_b = jnp.concatenate([w[:, o_ga:], w[:, o_hi:o_ga]], axis=1).astype(bf16)
    pf = _inproj(x2d, w_f, f32)
    pb = _inproj(x2d, w_b, bf16)

    o_a = _hgrn(pf, pb, hg_lb_logits, hg_norm_g[0].reshape(1, HG_WIDTH), batch, seq)
    o_b = _attn(pb, da_lambda[0], da_norm_g[0], batch, seq)

    h1, idx, tw, rank, cnt = _merge(
        o_a, o_b, pb, x2d, w_branch_a[0].astype(bf16), w_branch_b[0].astype(bf16), w_out[0].astype(bf16),
        ln1_g[0].reshape(1, d), ln1_b[0].reshape(1, d), router_w[0].astype(bf16), router_b[0].reshape(1, N_EXPERTS))

    n_rows_sorted = n * TOP_K + N_EXPERTS * MOE_T
    pos, tile_e, n_tiles, zrow, zflag = _routing_tables(cnt[0].astype(i32), idx, rank, n_rows_sorted // MOE_T)

    xs = _dispatch(h1, pos, zrow, zflag, n_rows_sorted)
    ys = _ffn(xs, tile_e, n_tiles, w_gate_up, w_down, b_gate_up[0][:, None, :], b_down[0][:, None, :])
    out = _combine(h1, tw, pos, ys, ln2_g[0].reshape(1, d), ln2_b[0].reshape(1, d))
    return out.reshape(batch, seq, d)
```

```python
import functools
import math

import jax
import jax.numpy as jnp
import numpy as np
from jax import lax
from jax.experimental import pallas as pl
from jax.experimental.pallas import tpu as pltpu

f32 = jnp.float32
bf16 = jnp.bfloat16
i32 = jnp.int32

D_MODEL = 1024
DEPTH = 1
HG_HEADS = 4
HG_DK = 128
HG_DV = 128
HG_KEY = HG_HEADS * HG_DK
HG_WIDTH = HG_HEADS * HG_DV
DA_HEADS = 4
DA_HD = 64
DA_DV = 2 * DA_HD
DA_QK = DA_HEADS * 2 * DA_HD
DA_WIDTH = DA_HEADS * DA_DV
N_EXPERTS = 32
TOP_K = 4
D_FF = 1024
SWIGLU_LIMIT = 7.0
SWIGLU_ALPHA = 1.702
DN_ALPHA = (2.0 * DEPTH) ** 0.25
LN_EPS = 1e-5
NORM_EPS = 1e-6
LAMBDA_INIT = 0.8 - 0.6 * math.exp(-0.3 * 0)

VMEM_LIMIT_BYTES = 52 * 1024 * 1024

PROJ_TM = 1024
PROJ_TN = 512
HG_CHUNK = 128
HG_ROWS = 512
ATT_T = 256
ATT_R = 128
MERGE_TM = 512
MOE_T = 256
ROW_TM = 256
FFN_FC = 512
LANES = 128
ROW_SUB = D_MODEL // LANES

PB_GA = 0
PB_GB = PB_GA + D_MODEL
PB_HI = PB_GB + D_MODEL
PB_HG = PB_HI + HG_WIDTH
PB_DQ = PB_HG + HG_WIDTH
PB_DK = PB_DQ + DA_QK
PB_DV = PB_DK + DA_QK


def _sigmoid(x):
    return 0.5 * jnp.tanh(0.5 * x) + 0.5


def _dot(a, b):
    return jnp.dot(a, b, preferred_element_type=f32)


def _dot_nt(a, b):
    return lax.dot_general(a, b, (((1,), (1,)), ((), ())), preferred_element_type=f32)


def _dot_tn(a, b):
    return lax.dot_general(a, b, (((0,), (0,)), ((), ())), preferred_element_type=f32)


def _inproj_kernel(x_ref, w_ref, o_ref, xb_ref):
    @pl.when(pl.program_id(1) == 0)
    def _():
        xb_ref[...] = x_ref[...].astype(bf16)

    o_ref[...] = _dot(xb_ref[...], w_ref[...]).astype(o_ref.dtype)


def _inproj(x2d, w, out_dtype):
    n, k = x2d.shape
    m = w.shape[1]
    tm = min(PROJ_TM, n)
    return pl.pallas_call(
        _inproj_kernel,
        grid=(n // tm, m // PROJ_TN),
        in_specs=[pl.BlockSpec((tm, k), lambda i, j: (i, 0)),
                  pl.BlockSpec((k, PROJ_TN), lambda i, j: (0, j))],
        out_specs=pl.BlockSpec((tm, PROJ_TN), lambda i, j: (i, j)),
        out_shape=jax.ShapeDtypeStruct((n, m), out_dtype),
        scratch_shapes=[pltpu.VMEM((tm, k), bf16)],
        compiler_params=pltpu.CompilerParams(
            dimension_semantics=("parallel", "arbitrary"), vmem_limit_bytes=VMEM_LIMIT_BYTES),
        name="inproj",
    )(x2d, w)


def _shift_down(x, d):
    n = x.shape[0]
    if d % 8 == 0:
        return jnp.concatenate([x[n - d:], x[:n - d]], axis=0)
    return pltpu.roll(x, d, axis=0)


def _shift_up(x, d):
    n = x.shape[0]
    if d % 8 == 0:
        return jnp.concatenate([x[d:], x[:d]], axis=0)
    return pltpu.roll(x, n - d, axis=0)


def _hgrn_chunk_head(qv, fl, v, g, lb, ng, state, tril, lvl, row, eye):
    c = qv.shape[0]
    f = lb + (1.0 - lb) * _sigmoid(fl)
    logf = jnp.log(f)
    kk = 1.0 - f
    qa = qv * _sigmoid(qv)

    l1 = logf.astype(bf16)
    r1 = logf - l1.astype(f32)
    l2 = r1.astype(bf16)
    l3 = (r1 - l2.astype(f32)).astype(bf16)
    bc = _dot(tril, jnp.concatenate([l1, l2, l3], axis=1))
    b = bc[:, 0:HG_DK] + bc[:, HG_DK:2 * HG_DK] + bc[:, 2 * HG_DK:3 * HG_DK]

    attn = jnp.where(lvl == -1, _dot_nt(qa.astype(bf16), kk.astype(bf16)), 0.0)
    filled = b
    d = 1
    level = 0
    while d < c:
        upper = (row & d) != 0
        ref_b = jnp.where(upper, _shift_down(filled, d), filled)
        diff = b - ref_b
        e = jnp.exp(jnp.where(upper, diff, -diff))
        mixed = (jnp.where(upper, qa, kk) * e).astype(bf16)
        attn = jnp.where(lvl == level, _dot_nt(mixed, mixed), attn)
        filled = jnp.where(upper, filled, _shift_up(filled, d))
        d *= 2
        level += 1

    o_intra = _dot(attn.astype(bf16), v)
    o_inter = _dot((qa * jnp.exp(b)).astype(bf16), state.astype(bf16))

    b_last = b[c - 1:c, :]
    k_dec = (kk * jnp.exp(b_last - b)).astype(bf16)
    dcol = jnp.sum(jnp.where(eye, jnp.broadcast_to(jnp.exp(b_last), eye.shape), 0.0), axis=1, keepdims=True)
    new_state = dcol * state + _dot_tn(k_dec, v)

    o = o_inter + o_intra
    ms = jnp.mean(o * o, axis=1, keepdims=True)
    o = o * lax.rsqrt(ms + NORM_EPS) * ng
    o = o * _sigmoid(g.astype(f32))
    return o.astype(bf16), new_state


def _hgrn_kernel(q_ref, f_ref, i_ref, g_ref, lbl_ref, ng_ref, tril_ref, lvl_ref, o_ref, st_ref, *, chunk, n_chunks):
    @pl.when(pl.program_id(1) == 0)
    def _():
        st_ref[...] = jnp.zeros_like(st_ref)

    lbl = lbl_ref[...]
    ex = jnp.exp(lbl - jnp.max(lbl, axis=0, keepdims=True))
    lb_all = ex[0:1, :] / jnp.sum(ex, axis=0, keepdims=True)
    ng_all = ng_ref[...]
    tril = tril_ref[...]
    lvl = lvl_ref[...]
    row = lax.broadcasted_iota(i32, (chunk, HG_DK), 0)
    eye = lax.broadcasted_iota(i32, (HG_DK, HG_DV), 0) == lax.broadcasted_iota(i32, (HG_DK, HG_DV), 1)

    def body(ci, carry):
        r0 = pl.multiple_of(ci * chunk, chunk)
        for h in range(HG_HEADS):
            cs = pl.ds(h * HG_DK, HG_DK)
            out, new_state = _hgrn_chunk_head(
                q_ref[pl.ds(r0, chunk), cs], f_ref[pl.ds(r0, chunk), cs],
                i_ref[pl.ds(r0, chunk), cs], g_ref[pl.ds(r0, chunk), cs],
                lb_all[:, h * HG_DK:(h + 1) * HG_DK], ng_all[:, h * HG_DV:(h + 1) * HG_DV],
                st_ref[h], tril, lvl, row, eye)
            o_ref[pl.ds(r0, chunk), cs] = out
            st_ref[h] = new_state
        return carry

    lax.fori_loop(0, n_chunks, body, 0)


def _hgrn_consts(chunk):
    t = np.arange(chunk)
    tril = (t[None, :] <= t[:, None]).astype(np.float32)
    x = t[:, None] ^ t[None, :]
    lvl = np.where(x > 0, np.floor(np.log2(np.maximum(x, 1))).astype(np.int32), -1)
    lvl = np.where(t[:, None] >= t[None, :], lvl, -2).astype(np.int32)
    return jnp.asarray(tril, dtype=bf16), jnp.asarray(lvl, dtype=i32)


def _hgrn(pf, pb, lb_logits, norm_g, batch, seq):
    n = batch * seq
    rows = min(HG_ROWS, seq)
    chunk = min(HG_CHUNK, rows)
    spb = seq // rows
    tril, lvl = _hgrn_consts(chunk)
    row_blk = lambda b, s: b * spb + s
    return pl.pallas_call(
        functools.partial(_hgrn_kernel, chunk=chunk, n_chunks=rows // chunk),
        grid=(batch, spb),
        in_specs=[
            pl.BlockSpec((rows, HG_KEY), lambda b, s: (row_blk(b, s), 0)),
            pl.BlockSpec((rows, HG_KEY), lambda b, s: (row_blk(b, s), 1)),
            pl.BlockSpec((rows, HG_WIDTH), lambda b, s: (row_blk(b, s), PB_HI // HG_WIDTH)),
            pl.BlockSpec((rows, HG_WIDTH), lambda b, s: (row_blk(b, s), PB_HG // HG_WIDTH)),
            pl.BlockSpec(lb_logits.shape, lambda b, s: (0, 0)),
            pl.BlockSpec((1, HG_WIDTH), lambda b, s: (0, 0)),
            pl.BlockSpec((chunk, chunk), lambda b, s: (0, 0)),
            pl.BlockSpec((chunk, chunk), lambda b, s: (0, 0)),
        ],
        out_specs=pl.BlockSpec((rows, HG_WIDTH), lambda b, s: (row_blk(b, s), 0)),
        out_shape=jax.ShapeDtypeStruct((n, HG_WIDTH), bf16),
        scratch_shapes=[pltpu.VMEM((HG_HEADS, HG_DK, HG_DV), f32)],
        compiler_params=pltpu.CompilerParams(
            dimension_semantics=("parallel", "arbitrary"), vmem_limit_bytes=VMEM_LIMIT_BYTES),
        name="hgrn2",
    )(pf, pf, pb, pb, lb_logits, norm_g, tril, lvl)


def _attn_kernel(lam_ref, ng_ref, q_ref, k_ref, v_ref, o_ref, qs_ref, m_ref, acc_ref, *, t):
    qi = pl.program_id(1)
    blk = 2 * DA_HD

    lane = lax.broadcasted_iota(i32, (t, blk), 1)
    for h in range(DA_HEADS):
        q = q_ref[:, h * blk:(h + 1) * blk] * jnp.asarray(DA_HD ** -0.5, bf16)
        zero = jnp.zeros_like(q)
        qs_ref[h, 0:t, :] = jnp.where(lane < DA_HD, q, zero)
        qs_ref[h, t:2 * t, :] = jnp.where(lane >= DA_HD, q, zero)
    m_ref[...] = jnp.full_like(m_ref, -jnp.inf)
    acc_ref[...] = jnp.zeros_like(acc_ref)

    col = lax.broadcasted_iota(i32, (1, t), 1)
    ones = jnp.ones((t, DA_DV), bf16)

    def step(kt, masked):
        k0 = pl.multiple_of(kt * t, t)
        rel = ((kt - qi) * t + col).astype(f32)
        if masked:
            rr = lax.broadcasted_iota(i32, (2 * t, t), 0)
            rr = jnp.where(rr >= t, rr - t, rr)
            causal = lax.broadcasted_iota(i32, (2 * t, t), 1) <= rr
        for h in range(DA_HEADS):
            cs = pl.ds(h * blk, blk)
            k = k_ref[pl.ds(k0, t), cs]
            vo = jnp.concatenate([v_ref[pl.ds(k0, t), cs], ones], axis=1)
            bias = (2.0 ** (-8.0 * (h + 1) / DA_HEADS)) * rel
            for r0 in range(0, 2 * t, ATT_R):
                rows = slice(r0, r0 + ATT_R)
                s = _dot_nt(qs_ref[h, rows, :], k) + bias
                if masked:
                    s = jnp.where(causal[rows, :], s, -jnp.inf)
                m_old = m_ref[h, rows, :]
                m_new = jnp.maximum(m_old, jnp.max(s, axis=1, keepdims=True))
                alpha = jnp.exp(m_old - m_new)
                p = jnp.exp(s - jnp.concatenate([m_new] * (t // DA_DV), axis=1)).astype(bf16)
                acc_ref[h, rows, :] = jnp.concatenate([alpha, alpha], axis=1) * acc_ref[h, rows, :] + _dot(p, vo)
                m_ref[h, rows, :] = m_new

    def loop_body(kt, carry):
        step(kt, False)
        return carry

    lax.fori_loop(0, qi, loop_body, 0)
    step(qi, True)

    lp = lam_ref[...]
    lam = (jnp.exp(jnp.sum(lp[0:1, :] * lp[1:2, :], axis=1, keepdims=True))
           - jnp.exp(jnp.sum(lp[2:3, :] * lp[3:4, :], axis=1, keepdims=True)) + LAMBDA_INIT)
    for h in range(DA_HEADS):
        acc = acc_ref[h]
        o_all = acc[:, 0:DA_DV] / acc[:, DA_DV:2 * DA_DV]
        o = o_all[0:t, :] - lam * o_all[t:2 * t, :]
        ms = jnp.mean(o * o, axis=1, keepdims=True)
        o = o * lax.rsqrt(ms + NORM_EPS) * ng_ref[:, h * DA_DV:(h + 1) * DA_DV] * (1.0 - LAMBDA_INIT)
        o_ref[:, h * DA_DV:(h + 1) * DA_DV] = o.astype(bf16)


def _attn(pb, lam_params, norm_g, batch, seq):
    n = batch * seq
    t = min(ATT_T, seq)
    nq = seq // t
    q0, k0, v0 = PB_DQ // DA_QK, PB_DK // DA_QK, PB_DV // DA_WIDTH
    return pl.pallas_call(
        functools.partial(_attn_kernel, t=t),
        grid=(batch, nq),
        in_specs=[
            pl.BlockSpec(lam_params.shape, lambda b, i: (0, 0)),
            pl.BlockSpec((1, DA_WIDTH), lambda b, i: (0, 0)),
            pl.BlockSpec((t, DA_QK), lambda b, i: (b * nq + i, q0)),
            pl.BlockSpec((seq, DA_QK), lambda b, i: (b, k0)),
            pl.BlockSpec((seq, DA_WIDTH), lambda b, i: (b, v0)),
        ],
        out_specs=pl.BlockSpec((t, DA_WIDTH), lambda b, i: (b * nq + i, 0)),
        out_shape=jax.ShapeDtypeStruct((n, DA_WIDTH), bf16),
        scratch_shapes=[pltpu.VMEM((DA_HEADS, 2 * t, 2 * DA_HD), bf16), pltpu.VMEM((DA_HEADS, 2 * t, DA_DV), f32),
                        pltpu.VMEM((DA_HEADS, 2 * t, 2 * DA_DV), f32)],
        compiler_params=pltpu.CompilerParams(
            dimension_semantics=("parallel", "arbitrary"), vmem_limit_bytes=VMEM_LIMIT_BYTES),
        name="diff_attn",
    )(lam_params, norm_g.reshape(1, DA_WIDTH), pb, pb, pb)


def _layer_norm(y, g, b):
    mu = jnp.mean(y, axis=1, keepdims=True)
    yc = y - mu
    var = jnp.mean(yc * yc, axis=1, keepdims=True)
    return yc * lax.rsqrt(var + LN_EPS) * g + b


def _assemble4(cols, dtype):
    tm = cols[0].shape[0]
    lane = lax.broadcasted_iota(i32, (tm, TOP_K), 1)
    out = jnp.broadcast_to(cols[TOP_K - 1], (tm, TOP_K))
    for k in range(TOP_K - 2, -1, -1):
        out = jnp.where(lane == k, jnp.broadcast_to(cols[k], (tm, TOP_K)), out)
    return out.astype(dtype)


def _merge_kernel(oa_ref, ob_ref, ga_ref, gb_ref, x_ref, wa_ref, wb_ref, wo_ref, g1_ref, b1_ref, rw_ref, rb_ref,
                  tri_ref, h1_ref, idx_ref, tw_ref, rank_ref, cnt_ref, carry_ref):
    @pl.when(pl.program_id(0) == 0)
    def _():
        carry_ref[...] = jnp.zeros_like(carry_ref)

    a = _dot(oa_ref[...], wa_ref[...])
    b = _dot(ob_ref[...], wb_ref[...])
    merged = _sigmoid(ga_ref[...].astype(f32)) * a + _sigmoid(gb_ref[...].astype(f32)) * b
    mix = _dot(merged.astype(bf16), wo_ref[...])
    h1 = _layer_norm(DN_ALPHA * x_ref[...] + mix, g1_ref[...], b1_ref[...])
    h1_ref[...] = h1

    logits = _dot(h1.astype(bf16), rw_ref[...]) + rb_ref[...]
    tm = logits.shape[0]
    lane = lax.broadcasted_iota(i32, (tm, N_EXPERTS), 1).astype(f32)
    work = logits
    vals, idxs = [], []
    for _ in range(TOP_K):
        mk = jnp.max(work, axis=1, keepdims=True)
        ik = jnp.min(jnp.where(work == mk, lane, float(N_EXPERTS)), axis=1, keepdims=True)
        vals.append(mk)
        idxs.append(ik)
        work = jnp.where(lane == ik, -jnp.inf, work)
    es = [jnp.exp(v - vals[0]) for v in vals]
    den = es[0] + es[1] + es[2] + es[3]
    tw_ref[...] = _assemble4([e / den for e in es], f32)
    idx_ref[...] = _assemble4(idxs, i32)

    onehot = jnp.zeros((tm, N_EXPERTS), f32)
    for ik in idxs:
        onehot = onehot + jnp.where(lane == ik, 1.0, 0.0)
    before = _dot(tri_ref[...], onehot.astype(bf16)) + carry_ref[...]
    ranks = [jnp.sum(jnp.where(lane == ik, before, 0.0), axis=1, keepdims=True) for ik in idxs]
    rank_ref[...] = _assemble4(ranks, i32)
    total = carry_ref[...] + jnp.sum(onehot, axis=0, keepdims=True)
    carry_ref[...] = total
    cnt_ref[...] = total


def _merge(o_a, o_b, pb, x2d, wa, wb, wo, g1, b1, rw, rb):
    n = x2d.shape[0]
    tm = min(MERGE_TM, n)
    t = np.arange(tm)
    tri = jnp.asarray((t[None, :] < t[:, None]).astype(np.float32), dtype=bf16)
    row = lambda i: (i, 0)
    const = lambda i: (0, 0)
    return pl.pallas_call(
        _merge_kernel,
        grid=(n // tm,),
        in_specs=[
            pl.BlockSpec((tm, HG_WIDTH), row),
            pl.BlockSpec((tm, DA_WIDTH), row),
            pl.BlockSpec((tm, D_MODEL), lambda i: (i, PB_GA // D_MODEL)),
            pl.BlockSpec((tm, D_MODEL), lambda i: (i, PB_GB // D_MODEL)),
            pl.BlockSpec((tm, D_MODEL), row),
            pl.BlockSpec(wa.shape, const), pl.BlockSpec(wb.shape, const), pl.BlockSpec(wo.shape, const),
            pl.BlockSpec(g1.shape, const), pl.BlockSpec(b1.shape, const),
            pl.BlockSpec(rw.shape, const), pl.BlockSpec(rb.shape, const),
            pl.BlockSpec((tm, tm), const),
        ],
        out_specs=[
            pl.BlockSpec((tm, D_MODEL), row),
            pl.BlockSpec((tm, TOP_K), row), pl.BlockSpec((tm, TOP_K), row), pl.BlockSpec((tm, TOP_K), row),
            pl.BlockSpec((1, N_EXPERTS), const),
        ],
        out_shape=[
            jax.ShapeDtypeStruct((n, D_MODEL), f32),
            jax.ShapeDtypeStruct((n, TOP_K), i32), jax.ShapeDtypeStruct((n, TOP_K), f32),
            jax.ShapeDtypeStruct((n, TOP_K), i32),
            jax.ShapeDtypeStruct((1, N_EXPERTS), f32),
        ],
        scratch_shapes=[pltpu.VMEM((1, N_EXPERTS), f32)],
        compiler_params=pltpu.CompilerParams(
            dimension_semantics=("arbitrary",), vmem_limit_bytes=VMEM_LIMIT_BYTES),
        name="merge_ln_router",
    )(o_a, o_b, pb, pb, x2d, wa, wb, wo, g1, b1, rw, rb, tri)


def _row_copy_loop(n_rows, start_one):
    def body(t, carry):
        for k in range(TOP_K):
            start_one(t, k)
        return carry
    lax.fori_loop(0, n_rows, body, 0)


def _to_row_tiles(dst_ref, x):
    rows = x.shape[0]
    for c in range(ROW_SUB):
        dst_ref[pl.ds(c, rows, stride=ROW_SUB), :] = x[:, c * LANES:(c + 1) * LANES]


def _from_row_tiles(src_ref, rows):
    return jnp.concatenate([src_ref[pl.ds(c, rows, stride=ROW_SUB), :] for c in range(ROW_SUB)], axis=1)


def _tile_rows(r, n=1):
    return pl.ds(pl.multiple_of(r * ROW_SUB, ROW_SUB), n * ROW_SUB)


def _dispatch_kernel(zrow_ref, zflag_ref, pos_ref, h1_ref, xs_ref, src_ref, zeros_ref, sem, zsem):
    tm = h1_ref.shape[0]

    @pl.when(pl.program_id(0) == 0)
    def _():
        zeros_ref[...] = jnp.zeros_like(zeros_ref)
        for e in range(2 * N_EXPERTS):
            @pl.when(zflag_ref[e] == 1)
            def _():
                cp = pltpu.make_async_copy(zeros_ref, xs_ref.at[_tile_rows(zrow_ref[e], MOE_T)], zsem)
                cp.start()
                cp.wait()

    _to_row_tiles(src_ref, h1_ref[...])

    def row_copy(t, k):
        return pltpu.make_async_copy(src_ref.at[_tile_rows(t)], xs_ref.at[_tile_rows(pos_ref[t * TOP_K + k])], sem)

    _row_copy_loop(tm, lambda t, k: row_copy(t, k).start())
    _row_copy_loop(tm, lambda t, k: row_copy(t, k).wait())


def _dispatch(h1, pos_flat, zrow, zflag, n_rows_sorted):
    n = h1.shape[0]
    tm = min(ROW_TM, n)
    return pl.pallas_call(
        _dispatch_kernel,
        grid_spec=pltpu.PrefetchScalarGridSpec(
            num_scalar_prefetch=2,
            grid=(n // tm,),
            in_specs=[
                pl.BlockSpec((tm * TOP_K,), lambda i, zr, zf: (i,), memory_space=pltpu.SMEM),
                pl.BlockSpec((tm, D_MODEL), lambda i, zr, zf: (i, 0)),
            ],
            out_specs=pl.BlockSpec(memory_space=pl.ANY),
            scratch_shapes=[pltpu.VMEM((tm * ROW_SUB, LANES), f32), pltpu.VMEM((MOE_T * ROW_SUB, LANES), f32),
                            pltpu.SemaphoreType.DMA, pltpu.SemaphoreType.DMA],
        ),
        out_shape=jax.ShapeDtypeStruct((n_rows_sorted * ROW_SUB, LANES), f32),
        compiler_params=pltpu.CompilerParams(
            dimension_semantics=("arbitrary",), vmem_limit_bytes=VMEM_LIMIT_BYTES),
        name="moe_dispatch",
    )(zrow, zflag, pos_flat, h1)


def _ffn_kernel(te_ref, nt_ref, x_ref, wgu_ref, wd_ref, bgu_ref, bd_ref, perm_ref, y_ref, wgu_s, wd_s):
    j = pl.program_id(0)

    @pl.when(j < nt_ref[0])
    def _():
        @pl.when(jnp.logical_or(j == 0, te_ref[j] != te_ref[jnp.maximum(j - 1, 0)]))
        def _():
            for r0 in range(0, D_MODEL, LANES):
                wgu_s[r0:r0 + LANES, :] = wgu_ref[0, r0:r0 + LANES, :].astype(bf16)
            for r0 in range(0, D_FF, LANES):
                wd_s[r0:r0 + LANES, :] = _dot(perm_ref[...], wd_ref[0, r0:r0 + LANES, :].astype(bf16)).astype(bf16)

        x = _from_row_tiles(x_ref, MOE_T).astype(bf16)
        acc = jnp.broadcast_to(bd_ref[0], (MOE_T, D_MODEL))
        even = (lax.broadcasted_iota(i32, (MOE_T, LANES), 1) & 1) == 0
        for c0 in range(0, D_FF, FFN_FC):
            cols = slice(2 * c0, 2 * (c0 + FFN_FC))
            hgu = _dot(x, wgu_s[:, cols]) + bgu_ref[0, :, cols]
            zs = []
            for j0 in range(0, 2 * FFN_FC, 2 * LANES):
                lo = hgu[:, j0:j0 + LANES]
                hi = hgu[:, j0 + LANES:j0 + 2 * LANES]
                gate = jnp.where(even, lo, pltpu.roll(hi, 1, axis=1))
                up = jnp.where(even, pltpu.roll(lo, LANES - 1, axis=1), hi)
                gate = jnp.minimum(gate, SWIGLU_LIMIT)
                up = jnp.clip(up, -SWIGLU_LIMIT, SWIGLU_LIMIT)
                glu = gate * _sigmoid(gate * SWIGLU_ALPHA)
                zs.append(((up + 1.0) * glu).astype(bf16))
            acc = acc + _dot(jnp.concatenate(zs, axis=1), wd_s[c0:c0 + FFN_FC, :])
        _to_row_tiles(y_ref, acc)

    @pl.when(j >= nt_ref[0])
    def _():
        y_ref[...] = jnp.zeros_like(y_ref)


def _ffn(xs, tile_e, n_tiles, wgu, wd, bgu, bd):
    p = xs.shape[0] // ROW_SUB
    last = lambda j, te, nt: jnp.minimum(j, nt[0] - 1)
    wmap = lambda j, te, nt: (0, te[j], 0, 0)
    bmap = lambda j, te, nt: (te[j], 0, 0)
    r = np.arange(LANES)
    perm = np.zeros((LANES, LANES), np.float32)
    perm[r, (r % 2) * (LANES // 2) + r // 2] = 1.0
    return pl.pallas_call(
        _ffn_kernel,
        grid_spec=pltpu.PrefetchScalarGridSpec(
            num_scalar_prefetch=2,
            grid=(p // MOE_T,),
            in_specs=[
                pl.BlockSpec((MOE_T * ROW_SUB, LANES), lambda j, te, nt: (last(j, te, nt), 0)),
                pl.BlockSpec((None, 1, D_MODEL, 2 * D_FF), wmap),
                pl.BlockSpec((None, 1, D_FF, D_MODEL), wmap),
                pl.BlockSpec((1, 1, 2 * D_FF), bmap),
                pl.BlockSpec((1, 1, D_MODEL), bmap),
                pl.BlockSpec((LANES, LANES), lambda j, te, nt: (0, 0)),
            ],
            out_specs=pl.BlockSpec((MOE_T * ROW_SUB, LANES), lambda j, te, nt: (j, 0)),
            scratch_shapes=[pltpu.VMEM((D_MODEL, 2 * D_FF), bf16), pltpu.VMEM((D_FF, D_MODEL), bf16)],
        ),
        out_shape=jax.ShapeDtypeStruct((p * ROW_SUB, LANES), f32),
        compiler_params=pltpu.CompilerParams(
            dimension_semantics=("arbitrary",), vmem_limit_bytes=VMEM_LIMIT_BYTES),
        name="moe_ffn",
    )(tile_e, n_tiles, xs, wgu, wd, bgu, bd, jnp.asarray(perm, dtype=bf16))


def _combine_kernel(pos_ref, h1_ref, tw_ref, g2_ref, b2_ref, ys_ref, o_ref, ybuf_ref, sem):
    tm = h1_ref.shape[0]

    def row_copy(t, k):
        return pltpu.make_async_copy(ys_ref.at[_tile_rows(pos_ref[t * TOP_K + k])], ybuf_ref.at[k, _tile_rows(t)], sem)

    _row_copy_loop(tm, lambda t, k: row_copy(t, k).start())
    _row_copy_loop(tm, lambda t, k: row_copy(t, k).wait())

    tw = tw_ref[...]
    ffn = tw[:, 0:1] * _from_row_tiles(ybuf_ref.at[0], tm)
    for k in range(1, TOP_K):
        ffn = ffn + tw[:, k:k + 1] * _from_row_tiles(ybuf_ref.at[k], tm)
    o_ref[...] = _layer_norm(DN_ALPHA * h1_ref[...] + ffn, g2_ref[...], b2_ref[...])


def _combine(h1, tw, pos_flat, ys, g2, b2):
    n = h1.shape[0]
    tm = min(ROW_TM, n)
    return pl.pallas_call(
        _combine_kernel,
        grid=(n // tm,),
        in_specs=[
            pl.BlockSpec((tm * TOP_K,), lambda i: (i,), memory_space=pltpu.SMEM),
            pl.BlockSpec((tm, D_MODEL), lambda i: (i, 0)),
            pl.BlockSpec((tm, TOP_K), lambda i: (i, 0)),
            pl.BlockSpec(g2.shape, lambda i: (0, 0)), pl.BlockSpec(b2.shape, lambda i: (0, 0)),
            pl.BlockSpec(memory_space=pl.ANY),
        ],
        out_specs=pl.BlockSpec((tm, D_MODEL), lambda i: (i, 0)),
        out_shape=jax.ShapeDtypeStruct((n, D_MODEL), f32),
        scratch_shapes=[pltpu.VMEM((TOP_K, tm * ROW_SUB, LANES), f32), pltpu.SemaphoreType.DMA],
        compiler_params=pltpu.CompilerParams(
            dimension_semantics=("arbitrary",), vmem_limit_bytes=VMEM_LIMIT_BYTES),
        name="moe_combine_ln",
    )(pos_flat, h1, tw, g2, b2, ys)


def _routing_tables(counts, idx, rank, n_tiles_max):
    padded = ((counts + MOE_T - 1) // MOE_T) * MOE_T
    ends = jnp.cumsum(padded)
    offs = ends - padded
    pos = (offs[idx] + rank).reshape(-1)
    n_tiles = (ends[-1] // MOE_T).astype(i32)
    tile_start = jnp.minimum(jnp.arange(n_tiles_max, dtype=i32), n_tiles - 1) * MOE_T
    tile_e = jnp.minimum(jnp.sum((ends[None, :] <= tile_start[:, None]).astype(i32), axis=1), N_EXPERTS - 1)
    trailing = jnp.arange(n_tiles_max - N_EXPERTS, n_tiles_max, dtype=i32)
    zrow = jnp.concatenate([(ends - MOE_T).astype(i32), trailing * MOE_T])
    zflag = jnp.concatenate([(padded > counts).astype(i32), (trailing >= n_tiles).astype(i32)])
    return pos.astype(i32), tile_e, n_tiles.reshape(1), zrow, zflag


def kernel(x, w_in, hg_lb_logits, hg_norm_g, da_lambda, da_norm_g, w_branch_a, w_branch_b, w_out, ln1_g, ln1_b,
           router_w, router_b, w_gate_up, b_gate_up, w_down, b_down, ln2_g, ln2_b):
    batch, seq, d = x.shape
    assert d == D_MODEL and w_in.shape[0] == DEPTH == 1
    n = batch * seq
    x2d = x.reshape(n, d)

    w = w_in[0]
    w_f = w[:, 0:2 * HG_KEY].astype(bf16)
    o_hi, o_dq, o_ga = 2 * HG_KEY, 2 * HG_KEY + 2 * HG_WIDTH, 2 * HG_KEY + 2 * HG_WIDTH + 2 * DA_QK + DA_WIDTH
    w_b = jnp.concatenate([w[:, o_ga:], w[:, o_hi:o_ga]], axis=1).astype(bf16)
    pf = _inproj(x2d, w_f, f32)
    pb = _inproj(x2d, w_b, bf16)

    o_a = _hgrn(pf, pb, hg_lb_logits, hg_norm_g[0].reshape(1, HG_WIDTH), batch, seq)
    o_b = _attn(pb, da_lambda[0], da_norm_g[0], batch, seq)

    h1, idx, tw, rank, cnt = _merge(
        o_a, o_b, pb, x2d, w_branch_a[0].astype(bf16), w_branch_b[0].astype(bf16), w_out[0].astype(bf16),
        ln1_g[0].reshape(1, d), ln1_b[0].reshape(1, d), router_w[0].astype(bf16), router_b[0].reshape(1, N_EXPERTS))

    n_rows_sorted = n * TOP_K + N_EXPERTS * MOE_T
    pos, tile_e, n_tiles, zrow, zflag = _routing_tables(cnt[0].astype(i32), idx, rank, n_rows_sorted // MOE_T)

    xs = _dispatch(h1, pos, zrow, zflag, n_rows_sorted)
    ys = _ffn(xs, tile_e, n_tiles, w_gate_up, w_down, b_gate_up[0][:, None, :], b_down[0][:, None, :])
    out = _combine(h1, tw, pos, ys, ln2_g[0].reshape(1, d), ln2_b[0].reshape(1, d))
    return out.reshape(batch, seq, d)
```

```python
import functools
import math

import jax
import jax.numpy as jnp
import numpy as np
from jax import lax
from jax.experimental import pallas as pl
from jax.experimental.pallas import tpu as pltpu

f32 = jnp.float32
bf16 = jnp.bfloat16
i32 = jnp.int32

D_MODEL = 1024
DEPTH = 1
HG_HEADS = 4
HG_DK = 128
HG_DV = 128
HG_KEY = HG_HEADS * HG_DK
HG_WIDTH = HG_HEADS * HG_DV
DA_HEADS = 4
DA_HD = 64
DA_DV = 2 * DA_HD
DA_QK = DA_HEADS * 2 * DA_HD
DA_WIDTH = DA_HEADS * DA_DV
N_EXPERTS = 32
TOP_K = 4
D_FF = 1024
SWIGLU_LIMIT = 7.0
SWIGLU_ALPHA = 1.702
DN_ALPHA = (2.0 * DEPTH) ** 0.25
LN_EPS = 1e-5
NORM_EPS = 1e-6
LAMBDA_INIT = 0.8 - 0.6 * math.exp(-0.3 * 0)

VMEM_LIMIT_BYTES = 52 * 1024 * 1024

PROJ_TM = 2048
PROJ_TN = 512
HG_CHUNK = 256
HG_ROWS = 512
ATT_T = 256
ATT_R = 128
MERGE_TM = 512
MOE_T = 256
ROW_TM = 256
FFN_FC = 512
LANES = 128
ROW_SUB = D_MODEL // LANES

PB_GA = 0
PB_GB = PB_GA + D_MODEL
PB_HI = PB_GB + D_MODEL
PB_HG = PB_HI + HG_WIDTH
PB_DQ = PB_HG + HG_WIDTH
PB_DK = PB_DQ + DA_QK
PB_DV = PB_DK + DA_QK


def _sigmoid(x):
    return 0.5 * jnp.tanh(0.5 * x) + 0.5


def _dot(a, b):
    return jnp.dot(a, b, preferred_element_type=f32)


def _dot_nt(a, b):
    return lax.dot_general(a, b, (((1,), (1,)), ((), ())), preferred_element_type=f32)


def _dot_tn(a, b):
    return lax.dot_general(a, b, (((0,), (0,)), ((), ())), preferred_element_type=f32)


def _inproj_kernel(x_ref, w_ref, of_ref, ob_ref, xb_ref, *, n_f32_tiles):
    j = pl.program_id(1)

    @pl.when(j == 0)
    def _():
        xb_ref[...] = x_ref[...].astype(bf16)

    r = _dot(xb_ref[...], w_ref[...])

    @pl.when(j < n_f32_tiles)
    def _():
        of_ref[...] = r

    @pl.when(j >= n_f32_tiles)
    def _():
        ob_ref[...] = r.astype(bf16)


def _inproj(x2d, w, n_f32_cols):
    n, k = x2d.shape
    m = w.shape[1]
    tm = min(PROJ_TM, n)
    nf = n_f32_cols // PROJ_TN
    return pl.pallas_call(
        functools.partial(_inproj_kernel, n_f32_tiles=nf),
        grid=(n // tm, m // PROJ_TN),
        in_specs=[pl.BlockSpec((tm, k), lambda i, j: (i, 0)),
                  pl.BlockSpec((k, PROJ_TN), lambda i, j: (0, j))],
        out_specs=[pl.BlockSpec((tm, PROJ_TN), lambda i, j: (i, jnp.minimum(j, nf - 1))),
                   pl.BlockSpec((tm, PROJ_TN), lambda i, j: (i, jnp.maximum(j - nf, 0)))],
        out_shape=[jax.ShapeDtypeStruct((n, n_f32_cols), f32), jax.ShapeDtypeStruct((n, m - n_f32_cols), bf16)],
        scratch_shapes=[pltpu.VMEM((tm, k), bf16)],
        compiler_params=pltpu.CompilerParams(
            dimension_semantics=("parallel", "arbitrary"), vmem_limit_bytes=VMEM_LIMIT_BYTES),
        name="inproj",
    )(x2d, w)


def _shift_down(x, d):
    n = x.shape[0]
    if d % 8 == 0:
        return jnp.concatenate([x[n - d:], x[:n - d]], axis=0)
    return pltpu.roll(x, d, axis=0)


def _shift_up(x, d):
    n = x.shape[0]
    if d % 8 == 0:
        return jnp.concatenate([x[d:], x[:d]], axis=0)
    return pltpu.roll(x, n - d, axis=0)


def _hgrn_chunk_head(qv, fl, v, g, lb, ng, state, tril, lvl, row, eye):
    c = qv.shape[0]
    f = lb + (1.0 - lb) * _sigmoid(fl)
    logf = jnp.log(f)
    kk = 1.0 - f
    qa = qv * _sigmoid(qv)

    l1 = logf.astype(bf16)
    r1 = logf - l1.astype(f32)
    l2 = r1.astype(bf16)
    l3 = (r1 - l2.astype(f32)).astype(bf16)
    bc = _dot(tril, jnp.concatenate([l1, l2, l3], axis=1))
    b = bc[:, 0:HG_DK] + bc[:, HG_DK:2 * HG_DK] + bc[:, 2 * HG_DK:3 * HG_DK]

    attn = jnp.where(lvl == -1, _dot_nt(qa.astype(bf16), kk.astype(bf16)), 0.0)
    filled = b
    d = 1
    level = 0
    while d < c:
        upper = (row & d) != 0
        ref_b = jnp.where(upper, _shift_down(filled, d), filled)
        diff = b - ref_b
        e = jnp.exp(jnp.where(upper, diff, -diff))
        mixed = (jnp.where(upper, qa, kk) * e).astype(bf16)
        attn = jnp.where(lvl == level, _dot_nt(mixed, mixed), attn)
        filled = jnp.where(upper, filled, _shift_up(filled, d))
        d *= 2
        level += 1

    o_intra = _dot(attn.astype(bf16), v)
    o_inter = _dot((qa * jnp.exp(b)).astype(bf16), state.astype(bf16))

    b_last = b[c - 1:c, :]
    k_dec = (kk * jnp.exp(b_last - b)).astype(bf16)
    dcol = jnp.sum(jnp.where(eye, jnp.broadcast_to(jnp.exp(b_last), eye.shape), 0.0), axis=1, keepdims=True)
    new_state = dcol * state + _dot_tn(k_dec, v)

    o = o_inter + o_intra
    ms = jnp.mean(o * o, axis=1, keepdims=True)
    o = o * lax.rsqrt(ms + NORM_EPS) * ng
    o = o * _sigmoid(g.astype(f32))
    return o.astype(bf16), new_state


def _hgrn_kernel(q_ref, f_ref, i_ref, g_ref, lbl_ref, ng_ref, tril_ref, lvl_ref, o_ref, st_ref, *, chunk, n_chunks):
    @pl.when(pl.program_id(1) == 0)
    def _():
        st_ref[...] = jnp.zeros_like(st_ref)

    lbl = lbl_ref[...]
    ex = jnp.exp(lbl - jnp.max(lbl, axis=0, keepdims=True))
    lb_all = ex[0:1, :] / jnp.sum(ex, axis=0, keepdims=True)
    ng_all = ng_ref[...]
    tril = tril_ref[...]
    lvl = lvl_ref[...]
    row = lax.broadcasted_iota(i32, (chunk, HG_DK), 0)
    eye = lax.broadcasted_iota(i32, (HG_DK, HG_DV), 0) == lax.broadcasted_iota(i32, (HG_DK, HG_DV), 1)

    def body(ci, carry):
        r0 = pl.multiple_of(ci * chunk, chunk)
        for h in range(HG_HEADS):
            cs = pl.ds(h * HG_DK, HG_DK)
            out, new_state = _hgrn_chunk_head(
                q_ref[pl.ds(r0, chunk), cs], f_ref[pl.ds(r0, chunk), cs],
                i_ref[pl.ds(r0, chunk), cs], g_ref[pl.ds(r0, chunk), cs],
                lb_all[:, h * HG_DK:(h + 1) * HG_DK], ng_all[:, h * HG_DV:(h + 1) * HG_DV],
                st_ref[h], tril, lvl, row, eye)
            o_ref[pl.ds(r0, chunk), cs] = out
            st_ref[h] = new_state
        return carry

    lax.fori_loop(0, n_chunks, body, 0)


def _hgrn_consts(chunk):
    t = np.arange(chunk)
    tril = (t[None, :] <= t[:, None]).astype(np.float32)
    x = t[:, None] ^ t[None, :]
    lvl = np.where(x > 0, np.floor(np.log2(np.maximum(x, 1))).astype(np.int32), -1)
    lvl = np.where(t[:, None] >= t[None, :], lvl, -2).astype(np.int32)
    return jnp.asarray(tril, dtype=bf16), jnp.asarray(lvl, dtype=i32)


def _hgrn(pf, pb, lb_logits, norm_g, batch, seq):
    n = batch * seq
    rows = min(HG_ROWS, seq)
    chunk = min(HG_CHUNK, rows)
    spb = seq // rows
    tril, lvl = _hgrn_consts(chunk)
    row_blk = lambda b, s: b * spb + s
    return pl.pallas_call(
        functools.partial(_hgrn_kernel, chunk=chunk, n_chunks=rows // chunk),
        grid=(batch, spb),
        in_specs=[
            pl.BlockSpec((rows, HG_KEY), lambda b, s: (row_blk(b, s), 0)),
            pl.BlockSpec((rows, HG_KEY), lambda b, s: (row_blk(b, s), 1)),
            pl.BlockSpec((rows, HG_WIDTH), lambda b, s: (row_blk(b, s), PB_HI // HG_WIDTH)),
            pl.BlockSpec((rows, HG_WIDTH), lambda b, s: (row_blk(b, s), PB_HG // HG_WIDTH)),
            pl.BlockSpec(lb_logits.shape, lambda b, s: (0, 0)),
            pl.BlockSpec((1, HG_WIDTH), lambda b, s: (0, 0)),
            pl.BlockSpec((chunk, chunk), lambda b, s: (0, 0)),
            pl.BlockSpec((chunk, chunk), lambda b, s: (0, 0)),
        ],
        out_specs=pl.BlockSpec((rows, HG_WIDTH), lambda b, s: (row_blk(b, s), 0)),
        out_shape=jax.ShapeDtypeStruct((n, HG_WIDTH), bf16),
        scratch_shapes=[pltpu.VMEM((HG_HEADS, HG_DK, HG_DV), f32)],
        compiler_params=pltpu.CompilerParams(
            dimension_semantics=("parallel", "arbitrary"), vmem_limit_bytes=VMEM_LIMIT_BYTES),
        name="hgrn2",
    )(pf, pf, pb, pb, lb_logits, norm_g, tril, lvl)


def _attn_kernel(lam_ref, ng_ref, q_ref, k_ref, v_ref, o_ref, qs_ref, m_ref, acc_ref, *, t):
    qi = pl.program_id(1)
    blk = 2 * DA_HD

    lane = lax.broadcasted_iota(i32, (t, blk), 1)
    for h in range(DA_HEADS):
        q = q_ref[:, h * blk:(h + 1) * blk] * jnp.asarray(DA_HD ** -0.5, bf16)
        zero = jnp.zeros_like(q)
        qs_ref[h, 0:t, :] = jnp.where(lane < DA_HD, q, zero)
        qs_ref[h, t:2 * t, :] = jnp.where(lane >= DA_HD, q, zero)
    m_ref[...] = jnp.full_like(m_ref, -jnp.inf)
    acc_ref[...] = jnp.zeros_like(acc_ref)

    col = lax.broadcasted_iota(i32, (1, t), 1)
    ones = jnp.ones((t, DA_DV), bf16)

    def step(kt, masked):
        k0 = pl.multiple_of(kt * t, t)
        rel = ((kt - qi) * t + col).astype(f32)
        if masked:
            rr = lax.broadcasted_iota(i32, (2 * t, t), 0)
            rr = jnp.where(rr >= t, rr - t, rr)
            causal = lax.broadcasted_iota(i32, (2 * t, t), 1) <= rr
        for h in range(DA_HEADS):
            cs = pl.ds(h * blk, blk)
            k = k_ref[pl.ds(k0, t), cs]
            vo = jnp.concatenate([v_ref[pl.ds(k0, t), cs], ones], axis=1)
            bias = (2.0 ** (-8.0 * (h + 1) / DA_HEADS)) * rel
            for r0 in range(0, 2 * t, ATT_R):
                rows = slice(r0, r0 + ATT_R)
                s = _dot_nt(qs_ref[h, rows, :], k) + bias
                if masked:
                    s = jnp.where(causal[rows, :], s, -jnp.inf)
                m_old = m_ref[h, rows, :]
                m_new = jnp.maximum(m_old, jnp.max(s, axis=1, keepdims=True))
                alpha = jnp.exp(m_old - m_new)
                p = jnp.exp(s - jnp.concatenate([m_new] * (t // DA_DV), axis=1)).astype(bf16)
                acc_ref[h, rows, :] = jnp.concatenate([alpha, alpha], axis=1) * acc_ref[h, rows, :] + _dot(p, vo)
                m_ref[h, rows, :] = m_new

    def loop_body(kt, carry):
        step(kt, False)
        return carry

    lax.fori_loop(0, qi, loop_body, 0)
    step(qi, True)

    lp = lam_ref[...]
    lam = (jnp.exp(jnp.sum(lp[0:1, :] * lp[1:2, :], axis=1, keepdims=True))
           - jnp.exp(jnp.sum(lp[2:3, :] * lp[3:4, :], axis=1, keepdims=True)) + LAMBDA_INIT)
    for h in range(DA_HEADS):
        acc = acc_ref[h]
        o_all = acc[:, 0:DA_DV] / acc[:, DA_DV:2 * DA_DV]
        o = o_all[0:t, :] - lam * o_all[t:2 * t, :]
        ms = jnp.mean(o * o, axis=1, keepdims=True)
        o = o * lax.rsqrt(ms + NORM_EPS) * ng_ref[:, h * DA_DV:(h + 1) * DA_DV] * (1.0 - LAMBDA_INIT)
        o_ref[:, h * DA_DV:(h + 1) * DA_DV] = o.astype(bf16)


def _attn(pb, lam_params, norm_g, batch, seq):
    n = batch * seq
    t = min(ATT_T, seq)
    nq = seq // t
    q0, k0, v0 = PB_DQ // DA_QK, PB_DK // DA_QK, PB_DV // DA_WIDTH
    return pl.pallas_call(
        functools.partial(_attn_kernel, t=t),
        grid=(batch, nq),
        in_specs=[
            pl.BlockSpec(lam_params.shape, lambda b, i: (0, 0)),
            pl.BlockSpec((1, DA_WIDTH), lambda b, i: (0, 0)),
            pl.BlockSpec((t, DA_QK), lambda b, i: (b * nq + i, q0)),
            pl.BlockSpec((seq, DA_QK), lambda b, i: (b, k0)),
            pl.BlockSpec((seq, DA_WIDTH), lambda b, i: (b, v0)),
        ],
        out_specs=pl.BlockSpec((t, DA_WIDTH), lambda b, i: (b * nq + i, 0)),
        out_shape=jax.ShapeDtypeStruct((n, DA_WIDTH), bf16),
        scratch_shapes=[pltpu.VMEM((DA_HEADS, 2 * t, 2 * DA_HD), bf16), pltpu.VMEM((DA_HEADS, 2 * t, DA_DV), f32),
                        pltpu.VMEM((DA_HEADS, 2 * t, 2 * DA_DV), f32)],
        compiler_params=pltpu.CompilerParams(
            dimension_semantics=("parallel", "arbitrary"), vmem_limit_bytes=VMEM_LIMIT_BYTES),
        name="diff_attn",
    )(lam_params, norm_g.reshape(1, DA_WIDTH), pb, pb, pb)


def _layer_norm(y, g, b):
    mu = jnp.mean(y, axis=1, keepdims=True)
    yc = y - mu
    var = jnp.mean(yc * yc, axis=1, keepdims=True)
    return yc * lax.rsqrt(var + LN_EPS) * g + b


def _assemble4(cols, dtype):
    tm = cols[0].shape[0]
    lane = lax.broadcasted_iota(i32, (tm, TOP_K), 1)
    out = jnp.broadcast_to(cols[TOP_K - 1], (tm, TOP_K))
    for k in range(TOP_K - 2, -1, -1):
        out = jnp.where(lane == k, jnp.broadcast_to(cols[k], (tm, TOP_K)), out)
    return out.astype(dtype)


def _merge_kernel(oa_ref, ob_ref, ga_ref, gb_ref, x_ref, wa_ref, wb_ref, wo_ref, g1_ref, b1_ref, rw_ref, rb_ref,
                  tri_ref, h1_ref, idx_ref, tw_ref, rank_ref, cnt_ref, carry_ref):
    @pl.when(pl.program_id(0) == 0)
    def _():
        carry_ref[...] = jnp.zeros_like(carry_ref)

    a = _dot(oa_ref[...], wa_ref[...])
    b = _dot(ob_ref[...], wb_ref[...])
    merged = _sigmoid(ga_ref[...].astype(f32)) * a + _sigmoid(gb_ref[...].astype(f32)) * b
    mix = _dot(merged.astype(bf16), wo_ref[...])
    h1 = _layer_norm(DN_ALPHA * x_ref[...] + mix, g1_ref[...], b1_ref[...])
    h1_ref[...] = h1

    logits = _dot(h1.astype(bf16), rw_ref[...]) + rb_ref[...]
    tm = logits.shape[0]
    lane = lax.broadcasted_iota(i32, (tm, N_EXPERTS), 1).astype(f32)
    work = logits
    vals, idxs = [], []
    for _ in range(TOP_K):
        mk = jnp.max(work, axis=1, keepdims=True)
        ik = jnp.min(jnp.where(work == mk, lane, float(N_EXPERTS)), axis=1, keepdims=True)
        vals.append(mk)
        idxs.append(ik)
        work = jnp.where(lane == ik, -jnp.inf, work)
    es = [jnp.exp(v - vals[0]) for v in vals]
    den = es[0] + es[1] + es[2] + es[3]
    tw_ref[...] = _assemble4([e / den for e in es], f32)
    idx_ref[...] = _assemble4(idxs, i32)

    onehot = jnp.zeros((tm, N_EXPERTS), f32)
    for ik in idxs:
        onehot = onehot + jnp.where(lane == ik, 1.0, 0.0)
    before = _dot(tri_ref[...], onehot.astype(bf16)) + carry_ref[...]
    ranks = [jnp.sum(jnp.where(lane == ik, before, 0.0), axis=1, keepdims=True) for ik in idxs]
    rank_ref[...] = _assemble4(ranks, i32)
    total = carry_ref[...] + jnp.sum(onehot, axis=0, keepdims=True)
    carry_ref[...] = total
    cnt_ref[...] = total


def _merge(o_a, o_b, pb, x2d, wa, wb, wo, g1, b1, rw, rb):
    n = x2d.shape[0]
    tm = min(MERGE_TM, n)
    t = np.arange(tm)
    tri = jnp.asarray((t[None, :] < t[:, None]).astype(np.float32), dtype=bf16)
    row = lambda i: (i, 0)
    const = lambda i: (0, 0)
    return pl.pallas_call(
        _merge_kernel,
        grid=(n // tm,),
        in_specs=[
            pl.BlockSpec((tm, HG_WIDTH), row),
            pl.BlockSpec((tm, DA_WIDTH), row),
            pl.BlockSpec((tm, D_MODEL), lambda i: (i, PB_GA // D_MODEL)),
            pl.BlockSpec((tm, D_MODEL), lambda i: (i, PB_GB // D_MODEL)),
            pl.BlockSpec((tm, D_MODEL), row),
            pl.BlockSpec(wa.shape, const), pl.BlockSpec(wb.shape, const), pl.BlockSpec(wo.shape, const),
            pl.BlockSpec(g1.shape, const), pl.BlockSpec(b1.shape, const),
            pl.BlockSpec(rw.shape, const), pl.BlockSpec(rb.shape, const),
            pl.BlockSpec((tm, tm), const),
        ],
        out_specs=[
            pl.BlockSpec((tm, D_MODEL), row),
            pl.BlockSpec((tm, TOP_K), row), pl.BlockSpec((tm, TOP_K), row), pl.BlockSpec((tm, TOP_K), row),
            pl.BlockSpec((1, N_EXPERTS), const),
        ],
        out_shape=[
            jax.ShapeDtypeStruct((n, D_MODEL), f32),
            jax.ShapeDtypeStruct((n, TOP_K), i32), jax.ShapeDtypeStruct((n, TOP_K), f32),
            jax.ShapeDtypeStruct((n, TOP_K), i32),
            jax.ShapeDtypeStruct((1, N_EXPERTS), f32),
        ],
        scratch_shapes=[pltpu.VMEM((1, N_EXPERTS), f32)],
        compiler_params=pltpu.CompilerParams(
            dimension_semantics=("arbitrary",), vmem_limit_bytes=VMEM_LIMIT_BYTES),
        name="merge_ln_router",
    )(o_a, o_b, pb, pb, x2d, wa, wb, wo, g1, b1, rw, rb, tri)


def _row_copy_loop(n_rows, start_one):
    def body(t, carry):
        for k in range(TOP_K):
            start_one(t, k)
        return carry
    lax.fori_loop(0, n_rows, body, 0)


def _to_row_tiles(dst_ref, x):
    rows = x.shape[0]
    for c in range(ROW_SUB):
        dst_ref[pl.ds(c, rows, stride=ROW_SUB), :] = x[:, c * LANES:(c + 1) * LANES]


def _from_row_tiles(src_ref, rows):
    return jnp.concatenate([src_ref[pl.ds(c, rows, stride=ROW_SUB), :] for c in range(ROW_SUB)], axis=1)


def _tile_rows(r, n=1):
    return pl.ds(pl.multiple_of(r * ROW_SUB, ROW_SUB), n * ROW_SUB)


def _dispatch_kernel(zrow_ref, zflag_ref, pos_ref, h1_ref, xs_ref, src_ref, zeros_ref, sems, zsem, *, n_steps):
    tm = h1_ref.shape[0]
    i = pl.program_id(0)
    slot = i % 2

    @pl.when(i == 0)
    def _():
        zeros_ref[...] = jnp.zeros_like(zeros_ref)
        for e in range(2 * N_EXPERTS):
            @pl.when(zflag_ref[e] == 1)
            def _():
                cp = pltpu.make_async_copy(zeros_ref, xs_ref.at[_tile_rows(zrow_ref[e], MOE_T)], zsem)
                cp.start()
                cp.wait()

    def row_copy(s, t, k):
        return pltpu.make_async_copy(src_ref.at[s, _tile_rows(t)], xs_ref.at[_tile_rows(pos_ref[t * TOP_K + k])],
                                     sems.at[s])

    def wait_slot(s):
        _row_copy_loop(tm, lambda t, k: row_copy(s, t, k).wait())

    @pl.when(i >= 2)
    def _():
        wait_slot(slot)

    _to_row_tiles(src_ref.at[slot], h1_ref[...])
    _row_copy_loop(tm, lambda t, k: row_copy(slot, t, k).start())

    @pl.when(i == n_steps - 1)
    def _():
        if n_steps >= 2:
            wait_slot(1 - slot)
        wait_slot(slot)


def _dispatch(h1, pos_flat, zrow, zflag, n_rows_sorted):
    n = h1.shape[0]
    tm = min(ROW_TM, n)
    return pl.pallas_call(
        functools.partial(_dispatch_kernel, n_steps=n // tm),
        grid_spec=pltpu.PrefetchScalarGridSpec(
            num_scalar_prefetch=2,
            grid=(n // tm,),
            in_specs=[
                pl.BlockSpec((tm * TOP_K,), lambda i, zr, zf: (i,), memory_space=pltpu.SMEM),
                pl.BlockSpec((tm, D_MODEL), lambda i, zr, zf: (i, 0)),
            ],
            out_specs=pl.BlockSpec(memory_space=pl.ANY),
            scratch_shapes=[pltpu.VMEM((2, tm * ROW_SUB, LANES), f32), pltpu.VMEM((MOE_T * ROW_SUB, LANES), f32),
                            pltpu.SemaphoreType.DMA((2,)), pltpu.SemaphoreType.DMA],
        ),
        out_shape=jax.ShapeDtypeStruct((n_rows_sorted * ROW_SUB, LANES), f32),
        compiler_params=pltpu.CompilerParams(
            dimension_semantics=("arbitrary",), vmem_limit_bytes=VMEM_LIMIT_BYTES),
        name="moe_dispatch",
    )(zrow, zflag, pos_flat, h1)


def _ffn_kernel(te_ref, nt_ref, x_ref, wgu_ref, wd_ref, bgu_ref, bd_ref, perm_ref, y_ref, wgu_s, wd_s):
    j = pl.program_id(0)

    @pl.when(j < nt_ref[0])
    def _():
        @pl.when(jnp.logical_or(j == 0, te_ref[j] != te_ref[jnp.maximum(j - 1, 0)]))
        def _():
            for r0 in range(0, D_MODEL, LANES):
                wgu_s[r0:r0 + LANES, :] = wgu_ref[0, r0:r0 + LANES, :].astype(bf16)
            for r0 in range(0, D_FF, LANES):
                wd_s[r0:r0 + LANES, :] = _dot(perm_ref[...], wd_ref[0, r0:r0 + LANES, :].astype(bf16)).astype(bf16)

        x = _from_row_tiles(x_ref, MOE_T).astype(bf16)
        acc = jnp.broadcast_to(bd_ref[0], (MOE_T, D_MODEL))
        even = (lax.broadcasted_iota(i32, (MOE_T, LANES), 1) & 1) == 0
        for c0 in range(0, D_FF, FFN_FC):
            cols = slice(2 * c0, 2 * (c0 + FFN_FC))
            hgu = _dot(x, wgu_s[:, cols]) + bgu_ref[0, :, cols]
            zs = []
            for j0 in range(0, 2 * FFN_FC, 2 * LANES):
                lo = hgu[:, j0:j0 + LANES]
                hi = hgu[:, j0 + LANES:j0 + 2 * LANES]
                gate = jnp.where(even, lo, pltpu.roll(hi, 1, axis=1))
                up = jnp.where(even, pltpu.roll(lo, LANES - 1, axis=1), hi)
                gate = jnp.minimum(gate, SWIGLU_LIMIT)
                up = jnp.clip(up, -SWIGLU_LIMIT, SWIGLU_LIMIT)
                glu = gate * _sigmoid(gate * SWIGLU_ALPHA)
                zs.append(((up + 1.0) * glu).astype(bf16))
            acc = acc + _dot(jnp.concatenate(zs, axis=1), wd_s[c0:c0 + FFN_FC, :])
        _to_row_tiles(y_ref, acc)

    @pl.when(j >= nt_ref[0])
    def _():
        y_ref[...] = jnp.zeros_like(y_ref)


def _ffn(xs, tile_e, n_tiles, wgu, wd, bgu, bd):
    p = xs.shape[0] // ROW_SUB
    last = lambda j, te, nt: jnp.minimum(j, nt[0] - 1)
    wmap = lambda j, te, nt: (0, te[j], 0, 0)
    bmap = lambda j, te, nt: (te[j], 0, 0)
    r = np.arange(LANES)
    perm = np.zeros((LANES, LANES), np.float32)
    perm[r, (r % 2) * (LANES // 2) + r // 2] = 1.0
    return pl.pallas_call(
        _ffn_kernel,
        grid_spec=pltpu.PrefetchScalarGridSpec(
            num_scalar_prefetch=2,
            grid=(p // MOE_T,),
            in_specs=[
                pl.BlockSpec((MOE_T * ROW_SUB, LANES), lambda j, te, nt: (last(j, te, nt), 0)),
                pl.BlockSpec((None, 1, D_MODEL, 2 * D_FF), wmap),
                pl.BlockSpec((None, 1, D_FF, D_MODEL), wmap),
                pl.BlockSpec((1, 1, 2 * D_FF), bmap),
                pl.BlockSpec((1, 1, D_MODEL), bmap),
                pl.BlockSpec((LANES, LANES), lambda j, te, nt: (0, 0)),
            ],
            out_specs=pl.BlockSpec((MOE_T * ROW_SUB, LANES), lambda j, te, nt: (j, 0)),
            scratch_shapes=[pltpu.VMEM((D_MODEL, 2 * D_FF), bf16), pltpu.VMEM((D_FF, D_MODEL), bf16)],
        ),
        out_shape=jax.ShapeDtypeStruct((p * ROW_SUB, LANES), f32),
        compiler_params=pltpu.CompilerParams(
            dimension_semantics=("arbitrary",), vmem_limit_bytes=VMEM_LIMIT_BYTES),
        name="moe_ffn",
    )(tile_e, n_tiles, xs, wgu, wd, bgu, bd, jnp.asarray(perm, dtype=bf16))


def _combine_kernel(pos_ref, nxt_ref, h1_ref, tw_ref, g2_ref, b2_ref, ys_ref, o_ref, ybuf_ref, sems, *, n_steps):
    tm = h1_ref.shape[0]
    i = pl.program_id(0)
    slot = i % 2

    def row_copy(p_ref, s, t, k):
        return pltpu.make_async_copy(ys_ref.at[_tile_rows(p_ref[t * TOP_K + k])], ybuf_ref.at[s, k, _tile_rows(t)],
                                     sems.at[s])

    @pl.when(i == 0)
    def _():
        _row_copy_loop(tm, lambda t, k: row_copy(pos_ref, 0, t, k).start())

    @pl.when(i + 1 < n_steps)
    def _():
        _row_copy_loop(tm, lambda t, k: row_copy(nxt_ref, 1 - slot, t, k).start())

    _row_copy_loop(tm, lambda t, k: row_copy(pos_ref, slot, t, k).wait())

    tw = tw_ref[...]
    ffn = tw[:, 0:1] * _from_row_tiles(ybuf_ref.at[slot, 0], tm)
    for k in range(1, TOP_K):
        ffn = ffn + tw[:, k:k + 1] * _from_row_tiles(ybuf_ref.at[slot, k], tm)
    o_ref[...] = _layer_norm(DN_ALPHA * h1_ref[...] + ffn, g2_ref[...], b2_ref[...])


def _combine(h1, tw, pos_flat, ys, g2, b2):
    n = h1.shape[0]
    tm = min(ROW_TM, n)
    n_steps = n // tm
    return pl.pallas_call(
        functools.partial(_combine_kernel, n_steps=n_steps),
        grid=(n_steps,),
        in_specs=[
            pl.BlockSpec((tm * TOP_K,), lambda i: (i,), memory_space=pltpu.SMEM),
            pl.BlockSpec((tm * TOP_K,), lambda i: (jnp.minimum(i + 1, n_steps - 1),), memory_space=pltpu.SMEM),
            pl.BlockSpec((tm, D_MODEL), lambda i: (i, 0)),
            pl.BlockSpec((tm, TOP_K), lambda i: (i, 0)),
            pl.BlockSpec(g2.shape, lambda i: (0, 0)), pl.BlockSpec(b2.shape, lambda i: (0, 0)),
            pl.BlockSpec(memory_space=pl.ANY),
        ],
        out_specs=pl.BlockSpec((tm, D_MODEL), lambda i: (i, 0)),
        out_shape=jax.ShapeDtypeStruct((n, D_MODEL), f32),
        scratch_shapes=[pltpu.VMEM((2, TOP_K, tm * ROW_SUB, LANES), f32), pltpu.SemaphoreType.DMA((2,))],
        compiler_params=pltpu.CompilerParams(
            dimension_semantics=("arbitrary",), vmem_limit_bytes=VMEM_LIMIT_BYTES),
        name="moe_combine_ln",
    )(pos_flat, pos_flat, h1, tw, g2, b2, ys)


def _routing_tables(counts, idx, rank, n_tiles_max):
    padded = ((counts + MOE_T - 1) // MOE_T) * MOE_T
    ends = jnp.cumsum(padded)
    offs = ends - padded
    pos = (offs[idx] + rank).reshape(-1)
    n_tiles = (ends[-1] // MOE_T).astype(i32)
    tile_start = jnp.minimum(jnp.arange(n_tiles_max, dtype=i32), n_tiles - 1) * MOE_T
    tile_e = jnp.minimum(jnp.sum((ends[None, :] <= tile_start[:, None]).astype(i32), axis=1), N_EXPERTS - 1)
    trailing = jnp.arange(n_tiles_max - N_EXPERTS, n_tiles_max, dtype=i32)
    zrow = jnp.concatenate([(ends - MOE_T).astype(i32), trailing * MOE_T])
    zflag = jnp.concatenate([(padded > counts).astype(i32), (trailing >= n_tiles).astype(i32)])
    return pos.astype(i32), tile_e, n_tiles.reshape(1), zrow, zflag


def kernel(x, w_in, hg_lb_logits, hg_norm_g, da_lambda, da_norm_g, w_branch_a, w_branch_b, w_out, ln1_g, ln1_b,
           router_w, router_b, w_gate_up, b_gate_up, w_down, b_down, ln2_g, ln2_b):
    batch, seq, d = x.shape
    assert d == D_MODEL and w_in.shape[0] == DEPTH == 1
    n = batch * seq
    x2d = x.reshape(n, d)

    w = w_in[0]
    o_hi, o_ga = 2 * HG_KEY, 2 * HG_KEY + 2 * HG_WIDTH + 2 * DA_QK + DA_WIDTH
    w_all = jnp.concatenate([w[:, 0:o_hi], w[:, o_ga:], w[:, o_hi:o_ga]], axis=1).astype(bf16)
    pf, pb = _inproj(x2d, w_all, o_hi)

    o_a = _hgrn(pf, pb, hg_lb_logits, hg_norm_g[0].reshape(1, HG_WIDTH), batch, seq)
    o_b = _attn(pb, da_lambda[0], da_norm_g[0], batch, seq)

    h1, idx, tw, rank, cnt = _merge(
        o_a, o_b, pb, x2d, w_branch_a[0].astype(bf16), w_branch_b[0].astype(bf16), w_out[0].astype(bf16),
        ln1_g[0].reshape(1, d), ln1_b[0].reshape(1, d), router_w[0].astype(bf16), router_b[0].reshape(1, N_EXPERTS))

    n_rows_sorted = n * TOP_K + N_EXPERTS * MOE_T
    pos, tile_e, n_tiles, zrow, zflag = _routing_tables(cnt[0].astype(i32), idx, rank, n_rows_sorted // MOE_T)

    xs = _dispatch(h1, pos, zrow, zflag, n_rows_sorted)
    ys = _ffn(xs, tile_e, n_tiles, w_gate_up, w_down, b_gate_up[0][:, None, :], b_down[0][:, None, :])
    out = _combine(h1, tw, pos, ys, ln2_g[0].reshape(1, d), ln2_b[0].reshape(1, d))
    return out.reshape(batch, seq, d)
```

```python
import functools
import math

import jax
import jax.numpy as jnp
import numpy as np
from jax import lax
from jax.experimental import pallas as pl
from jax.experimental.pallas import tpu as pltpu

f32 = jnp.float32
bf16 = jnp.bfloat16
i32 = jnp.int32

D_MODEL = 1024
DEPTH = 1
HG_HEADS = 4
HG_DK = 128
HG_DV = 128
HG_KEY = HG_HEADS * HG_DK
HG_WIDTH = HG_HEADS * HG_DV
DA_HEADS = 4
DA_HD = 64
DA_DV = 2 * DA_HD
DA_QK = DA_HEADS * 2 * DA_HD
DA_WIDTH = DA_HEADS * DA_DV
N_EXPERTS = 32
TOP_K = 4
D_FF = 1024
SWIGLU_LIMIT = 7.0
SWIGLU_ALPHA = 1.702
DN_ALPHA = (2.0 * DEPTH) ** 0.25
LN_EPS = 1e-5
NORM_EPS = 1e-6
LAMBDA_INIT = 0.8 - 0.6 * math.exp(-0.3 * 0)

VMEM_LIMIT_BYTES = 52 * 1024 * 1024

PROJ_TM = 2048
PROJ_TN = 512
HG_CHUNK = 256
HG_ROWS = 512
ATT_T = 256
ATT_R = 128
MERGE_TM = 512
MOE_T = 256
ROW_TM = 256
FFN_FC = 1024
LANES = 128
ROW_SUB = D_MODEL // LANES

PB_GA = 0
PB_GB = PB_GA + D_MODEL
PB_HI = PB_GB + D_MODEL
PB_HG = PB_HI + HG_WIDTH
PB_DQ = PB_HG + HG_WIDTH
PB_DK = PB_DQ + DA_QK
PB_DV = PB_DK + DA_QK


def _sigmoid(x):
    return 0.5 * jnp.tanh(0.5 * x) + 0.5


def _dot(a, b):
    return jnp.dot(a, b, preferred_element_type=f32)


def _dot_nt(a, b):
    return lax.dot_general(a, b, (((1,), (1,)), ((), ())), preferred_element_type=f32)


def _dot_tn(a, b):
    return lax.dot_general(a, b, (((0,), (0,)), ((), ())), preferred_element_type=f32)


def _inproj_kernel(x_ref, w_ref, of_ref, ob_ref, xb_ref, *, n_f32_tiles):
    j = pl.program_id(1)

    @pl.when(j == 0)
    def _():
        xb_ref[...] = x_ref[...].astype(bf16)

    r = _dot(xb_ref[...], w_ref[...])

    @pl.when(j < n_f32_tiles)
    def _():
        of_ref[...] = r

    @pl.when(j >= n_f32_tiles)
    def _():
        ob_ref[...] = r.astype(bf16)


def _inproj(x2d, w, n_f32_cols):
    n, k = x2d.shape
    m = w.shape[1]
    tm = min(PROJ_TM, n)
    nf = n_f32_cols // PROJ_TN
    return pl.pallas_call(
        functools.partial(_inproj_kernel, n_f32_tiles=nf),
        grid=(n // tm, m // PROJ_TN),
        in_specs=[pl.BlockSpec((tm, k), lambda i, j: (i, 0)),
                  pl.BlockSpec((k, PROJ_TN), lambda i, j: (0, j))],
        out_specs=[pl.BlockSpec((tm, PROJ_TN), lambda i, j: (i, jnp.minimum(j, nf - 1))),
                   pl.BlockSpec((tm, PROJ_TN), lambda i, j: (i, jnp.maximum(j - nf, 0)))],
        out_shape=[jax.ShapeDtypeStruct((n, n_f32_cols), f32), jax.ShapeDtypeStruct((n, m - n_f32_cols), bf16)],
        scratch_shapes=[pltpu.VMEM((tm, k), bf16)],
        compiler_params=pltpu.CompilerParams(
            dimension_semantics=("parallel", "arbitrary"), vmem_limit_bytes=VMEM_LIMIT_BYTES),
        name="inproj",
    )(x2d, w)


def _shift_down(x, d):
    n = x.shape[0]
    if d % 8 == 0:
        return jnp.concatenate([x[n - d:], x[:n - d]], axis=0)
    return pltpu.roll(x, d, axis=0)


def _shift_up(x, d):
    n = x.shape[0]
    if d % 8 == 0:
        return jnp.concatenate([x[d:], x[:d]], axis=0)
    return pltpu.roll(x, n - d, axis=0)


def _hgrn_chunk_head(qv, fl, v, g, lb, ng, state, tril, lvl, row, eye):
    c = qv.shape[0]
    f = lb + (1.0 - lb) * _sigmoid(fl)
    logf = jnp.log(f)
    kk = 1.0 - f
    qa = qv * _sigmoid(qv)

    l1 = logf.astype(bf16)
    r1 = logf - l1.astype(f32)
    l2 = r1.astype(bf16)
    l3 = (r1 - l2.astype(f32)).astype(bf16)
    bc = _dot(tril, jnp.concatenate([l1, l2, l3], axis=1))
    b = bc[:, 0:HG_DK] + bc[:, HG_DK:2 * HG_DK] + bc[:, 2 * HG_DK:3 * HG_DK]

    attn = jnp.where(lvl == -1, _dot_nt(qa.astype(bf16), kk.astype(bf16)), 0.0)
    filled = b
    d = 1
    level = 0
    while d < c:
        upper = (row & d) != 0
        ref_b = jnp.where(upper, _shift_down(filled, d), filled)
        diff = b - ref_b
        e = jnp.exp(jnp.where(upper, diff, -diff))
        mixed = (jnp.where(upper, qa, kk) * e).astype(bf16)
        attn = jnp.where(lvl == level, _dot_nt(mixed, mixed), attn)
        filled = jnp.where(upper, filled, _shift_up(filled, d))
        d *= 2
        level += 1

    o_intra = _dot(attn.astype(bf16), v)
    o_inter = _dot((qa * jnp.exp(b)).astype(bf16), state.astype(bf16))

    b_last = b[c - 1:c, :]
    k_dec = (kk * jnp.exp(b_last - b)).astype(bf16)
    dcol = jnp.sum(jnp.where(eye, jnp.broadcast_to(jnp.exp(b_last), eye.shape), 0.0), axis=1, keepdims=True)
    new_state = dcol * state + _dot_tn(k_dec, v)

    o = o_inter + o_intra
    ms = jnp.mean(o * o, axis=1, keepdims=True)
    o = o * lax.rsqrt(ms + NORM_EPS) * ng
    o = o * _sigmoid(g.astype(f32))
    return o.astype(bf16), new_state


def _hgrn_kernel(q_ref, f_ref, i_ref, g_ref, lbl_ref, ng_ref, tril_ref, lvl_ref, o_ref, st_ref, *, chunk, n_chunks):
    @pl.when(pl.program_id(1) == 0)
    def _():
        st_ref[...] = jnp.zeros_like(st_ref)

    lbl = lbl_ref[...]
    ex = jnp.exp(lbl - jnp.max(lbl, axis=0, keepdims=True))
    lb_all = ex[0:1, :] / jnp.sum(ex, axis=0, keepdims=True)
    ng_all = ng_ref[...]
    tril = tril_ref[...]
    lvl = lvl_ref[...]
    row = lax.broadcasted_iota(i32, (chunk, HG_DK), 0)
    eye = lax.broadcasted_iota(i32, (HG_DK, HG_DV), 0) == lax.broadcasted_iota(i32, (HG_DK, HG_DV), 1)

    def body(ci, carry):
        r0 = pl.multiple_of(ci * chunk, chunk)
        for h in range(HG_HEADS):
            cs = pl.ds(h * HG_DK, HG_DK)
            out, new_state = _hgrn_chunk_head(
                q_ref[pl.ds(r0, chunk), cs], f_ref[pl.ds(r0, chunk), cs],
                i_ref[pl.ds(r0, chunk), cs], g_ref[pl.ds(r0, chunk), cs],
                lb_all[:, h * HG_DK:(h + 1) * HG_DK], ng_all[:, h * HG_DV:(h + 1) * HG_DV],
                st_ref[h], tril, lvl, row, eye)
            o_ref[pl.ds(r0, chunk), cs] = out
            st_ref[h] = new_state
        return carry

    lax.fori_loop(0, n_chunks, body, 0)


def _hgrn_consts(chunk):
    t = np.arange(chunk)
    tril = (t[None, :] <= t[:, None]).astype(np.float32)
    x = t[:, None] ^ t[None, :]
    lvl = np.where(x > 0, np.floor(np.log2(np.maximum(x, 1))).astype(np.int32), -1)
    lvl = np.where(t[:, None] >= t[None, :], lvl, -2).astype(np.int32)
    return jnp.asarray(tril, dtype=bf16), jnp.asarray(lvl, dtype=i32)


def _hgrn(pf, pb, lb_logits, norm_g, batch, seq):
    n = batch * seq
    rows = min(HG_ROWS, seq)
    chunk = min(HG_CHUNK, rows)
    spb = seq // rows
    tril, lvl = _hgrn_consts(chunk)
    row_blk = lambda b, s: b * spb + s
    return pl.pallas_call(
        functools.partial(_hgrn_kernel, chunk=chunk, n_chunks=rows // chunk),
        grid=(batch, spb),
        in_specs=[
            pl.BlockSpec((rows, HG_KEY), lambda b, s: (row_blk(b, s), 0)),
            pl.BlockSpec((rows, HG_KEY), lambda b, s: (row_blk(b, s), 1)),
            pl.BlockSpec((rows, HG_WIDTH), lambda b, s: (row_blk(b, s), PB_HI // HG_WIDTH)),
            pl.BlockSpec((rows, HG_WIDTH), lambda b, s: (row_blk(b, s), PB_HG // HG_WIDTH)),
            pl.BlockSpec(lb_logits.shape, lambda b, s: (0, 0)),
            pl.BlockSpec((1, HG_WIDTH), lambda b, s: (0, 0)),
            pl.BlockSpec((chunk, chunk), lambda b, s: (0, 0)),
            pl.BlockSpec((chunk, chunk), lambda b, s: (0, 0)),
        ],
        out_specs=pl.BlockSpec((rows, HG_WIDTH), lambda b, s: (row_blk(b, s), 0)),
        out_shape=jax.ShapeDtypeStruct((n, HG_WIDTH), bf16),
        scratch_shapes=[pltpu.VMEM((HG_HEADS, HG_DK, HG_DV), f32)],
        compiler_params=pltpu.CompilerParams(
            dimension_semantics=("parallel", "arbitrary"), vmem_limit_bytes=VMEM_LIMIT_BYTES),
        name="hgrn2",
    )(pf, pf, pb, pb, lb_logits, norm_g, tril, lvl)


def _attn_kernel(lam_ref, ng_ref, q_ref, k_ref, v_ref, o_ref, qs_ref, m_ref, acc_ref, *, t):
    qi = pl.program_id(1)
    blk = 2 * DA_HD

    lane = lax.broadcasted_iota(i32, (t, blk), 1)
    for h in range(DA_HEADS):
        q = q_ref[:, h * blk:(h + 1) * blk] * jnp.asarray(DA_HD ** -0.5, bf16)
        zero = jnp.zeros_like(q)
        qs_ref[h, 0:t, :] = jnp.where(lane < DA_HD, q, zero)
        qs_ref[h, t:2 * t, :] = jnp.where(lane >= DA_HD, q, zero)
    m_ref[...] = jnp.full_like(m_ref, -jnp.inf)
    acc_ref[...] = jnp.zeros_like(acc_ref)

    col = lax.broadcasted_iota(i32, (1, t), 1)
    ones = jnp.ones((t, DA_DV), bf16)

    def step(kt, masked):
        k0 = pl.multiple_of(kt * t, t)
        rel = ((kt - qi) * t + col).astype(f32)
        if masked:
            rr = lax.broadcasted_iota(i32, (2 * t, t), 0)
            rr = jnp.where(rr >= t, rr - t, rr)
            causal = lax.broadcasted_iota(i32, (2 * t, t), 1) <= rr
        for h in range(DA_HEADS):
            cs = pl.ds(h * blk, blk)
            k = k_ref[pl.ds(k0, t), cs]
            vo = jnp.concatenate([v_ref[pl.ds(k0, t), cs], ones], axis=1)
            bias = (2.0 ** (-8.0 * (h + 1) / DA_HEADS)) * rel
            for r0 in range(0, 2 * t, ATT_R):
                rows = slice(r0, r0 + ATT_R)
                s = _dot_nt(qs_ref[h, rows, :], k) + bias
                if masked:
                    s = jnp.where(causal[rows, :], s, -jnp.inf)
                m_old = m_ref[h, rows, :]
                m_new = jnp.maximum(m_old, jnp.max(s, axis=1, keepdims=True))
                alpha = jnp.exp(m_old - m_new)
                p = jnp.exp(s - jnp.concatenate([m_new] * (t // DA_DV), axis=1)).astype(bf16)
                acc_ref[h, rows, :] = jnp.concatenate([alpha, alpha], axis=1) * acc_ref[h, rows, :] + _dot(p, vo)
                m_ref[h, rows, :] = m_new

    def loop_body(kp, carry):
        step(2 * kp, False)
        step(2 * kp + 1, False)
        return carry

    lax.fori_loop(0, qi // 2, loop_body, 0)

    @pl.when(qi % 2 == 1)
    def _():
        step(qi - 1, False)

    step(qi, True)

    lp = lam_ref[...]
    lam = (jnp.exp(jnp.sum(lp[0:1, :] * lp[1:2, :], axis=1, keepdims=True))
           - jnp.exp(jnp.sum(lp[2:3, :] * lp[3:4, :], axis=1, keepdims=True)) + LAMBDA_INIT)
    for h in range(DA_HEADS):
        acc = acc_ref[h]
        o_all = acc[:, 0:DA_DV] / acc[:, DA_DV:2 * DA_DV]
        o = o_all[0:t, :] - lam * o_all[t:2 * t, :]
        ms = jnp.mean(o * o, axis=1, keepdims=True)
        o = o * lax.rsqrt(ms + NORM_EPS) * ng_ref[:, h * DA_DV:(h + 1) * DA_DV] * (1.0 - LAMBDA_INIT)
        o_ref[:, h * DA_DV:(h + 1) * DA_DV] = o.astype(bf16)


def _attn(pb, lam_params, norm_g, batch, seq):
    n = batch * seq
    t = min(ATT_T, seq)
    nq = seq // t
    q0, k0, v0 = PB_DQ // DA_QK, PB_DK // DA_QK, PB_DV // DA_WIDTH
    return pl.pallas_call(
        functools.partial(_attn_kernel, t=t),
        grid=(batch, nq),
        in_specs=[
            pl.BlockSpec(lam_params.shape, lambda b, i: (0, 0)),
            pl.BlockSpec((1, DA_WIDTH), lambda b, i: (0, 0)),
            pl.BlockSpec((t, DA_QK), lambda b, i: (b * nq + i, q0)),
            pl.BlockSpec((seq, DA_QK), lambda b, i: (b, k0)),
            pl.BlockSpec((seq, DA_WIDTH), lambda b, i: (b, v0)),
        ],
        out_specs=pl.BlockSpec((t, DA_WIDTH), lambda b, i: (b * nq + i, 0)),
        out_shape=jax.ShapeDtypeStruct((n, DA_WIDTH), bf16),
        scratch_shapes=[pltpu.VMEM((DA_HEADS, 2 * t, 2 * DA_HD), bf16), pltpu.VMEM((DA_HEADS, 2 * t, DA_DV), f32),
                        pltpu.VMEM((DA_HEADS, 2 * t, 2 * DA_DV), f32)],
        compiler_params=pltpu.CompilerParams(
            dimension_semantics=("parallel", "arbitrary"), vmem_limit_bytes=VMEM_LIMIT_BYTES),
        name="diff_attn",
    )(lam_params, norm_g.reshape(1, DA_WIDTH), pb, pb, pb)


def _layer_norm(y, g, b):
    mu = jnp.mean(y, axis=1, keepdims=True)
    yc = y - mu
    var = jnp.mean(yc * yc, axis=1, keepdims=True)
    return yc * lax.rsqrt(var + LN_EPS) * g + b


def _assemble4(cols, dtype):
    tm = cols[0].shape[0]
    lane = lax.broadcasted_iota(i32, (tm, TOP_K), 1)
    out = jnp.broadcast_to(cols[TOP_K - 1], (tm, TOP_K))
    for k in range(TOP_K - 2, -1, -1):
        out = jnp.where(lane == k, jnp.broadcast_to(cols[k], (tm, TOP_K)), out)
    return out.astype(dtype)


def _merge_kernel(oa_ref, ob_ref, ga_ref, gb_ref, x_ref, wa_ref, wb_ref, wo_ref, g1_ref, b1_ref, rw_ref, rb_ref,
                  tri_ref, h1_ref, idx_ref, tw_ref, rank_ref, cnt_ref, carry_ref):
    @pl.when(pl.program_id(0) == 0)
    def _():
        carry_ref[...] = jnp.zeros_like(carry_ref)

    a = _dot(oa_ref[...], wa_ref[...])
    b = _dot(ob_ref[...], wb_ref[...])
    merged = _sigmoid(ga_ref[...].astype(f32)) * a + _sigmoid(gb_ref[...].astype(f32)) * b
    mix = _dot(merged.astype(bf16), wo_ref[...])
    h1 = _layer_norm(DN_ALPHA * x_ref[...] + mix, g1_ref[...], b1_ref[...])
    h1_ref[...] = h1

    logits = _dot(h1.astype(bf16), rw_ref[...]) + rb_ref[...]
    tm = logits.shape[0]
    lane = lax.broadcasted_iota(i32, (tm, N_EXPERTS), 1).astype(f32)
    work = logits
    vals, idxs = [], []
    for _ in range(TOP_K):
        mk = jnp.max(work, axis=1, keepdims=True)
        ik = jnp.min(jnp.where(work == mk, lane, float(N_EXPERTS)), axis=1, keepdims=True)
        vals.append(mk)
        idxs.append(ik)
        work = jnp.where(lane == ik, -jnp.inf, work)
    es = [jnp.exp(v - vals[0]) for v in vals]
    den = es[0] + es[1] + es[2] + es[3]
    tw_ref[...] = _assemble4([e / den for e in es], f32)
    idx_ref[...] = _assemble4(idxs, i32)

    onehot = jnp.zeros((tm, N_EXPERTS), f32)
    for ik in idxs:
        onehot = onehot + jnp.where(lane == ik, 1.0, 0.0)
    before = _dot(tri_ref[...], onehot.astype(bf16)) + carry_ref[...]
    ranks = [jnp.sum(jnp.where(lane == ik, before, 0.0), axis=1, keepdims=True) for ik in idxs]
    rank_ref[...] = _assemble4(ranks, i32)
    total = carry_ref[...] + jnp.sum(onehot, axis=0, keepdims=True)
    carry_ref[...] = total
    cnt_ref[...] = total


def _merge(o_a, o_b, pb, x2d, wa, wb, wo, g1, b1, rw, rb):
    n = x2d.shape[0]
    tm = min(MERGE_TM, n)
    t = np.arange(tm)
    tri = jnp.asarray((t[None, :] < t[:, None]).astype(np.float32), dtype=bf16)
    row = lambda i: (i, 0)
    const = lambda i: (0, 0)
    return pl.pallas_call(
        _merge_kernel,
        grid=(n // tm,),
        in_specs=[
            pl.BlockSpec((tm, HG_WIDTH), row),
            pl.BlockSpec((tm, DA_WIDTH), row),
            pl.BlockSpec((tm, D_MODEL), lambda i: (i, PB_GA // D_MODEL)),
            pl.BlockSpec((tm, D_MODEL), lambda i: (i, PB_GB // D_MODEL)),
            pl.BlockSpec((tm, D_MODEL), row),
            pl.BlockSpec(wa.shape, const), pl.BlockSpec(wb.shape, const), pl.BlockSpec(wo.shape, const),
            pl.BlockSpec(g1.shape, const), pl.BlockSpec(b1.shape, const),
            pl.BlockSpec(rw.shape, const), pl.BlockSpec(rb.shape, const),
            pl.BlockSpec((tm, tm), const),
        ],
        out_specs=[
            pl.BlockSpec((tm, D_MODEL), row),
            pl.BlockSpec((tm, TOP_K), row), pl.BlockSpec((tm, TOP_K), row), pl.BlockSpec((tm, TOP_K), row),
            pl.BlockSpec((1, N_EXPERTS), const),
        ],
        out_shape=[
            jax.ShapeDtypeStruct((n, D_MODEL), f32),
            jax.ShapeDtypeStruct((n, TOP_K), i32), jax.ShapeDtypeStruct((n, TOP_K), f32),
            jax.ShapeDtypeStruct((n, TOP_K), i32),
            jax.ShapeDtypeStruct((1, N_EXPERTS), f32),
        ],
        scratch_shapes=[pltpu.VMEM((1, N_EXPERTS), f32)],
        compiler_params=pltpu.CompilerParams(
            dimension_semantics=("arbitrary",), vmem_limit_bytes=VMEM_LIMIT_BYTES),
        name="merge_ln_router",
    )(o_a, o_b, pb, pb, x2d, wa, wb, wo, g1, b1, rw, rb, tri)


def _row_copy_loop(n_rows, start_one):
    def body(t, carry):
        for k in range(TOP_K):
            start_one(t, k)
        return carry
    lax.fori_loop(0, n_rows, body, 0)


def _to_row_tiles(dst_ref, x):
    rows = x.shape[0]
    for c in range(ROW_SUB):
        dst_ref[pl.ds(c, rows, stride=ROW_SUB), :] = x[:, c * LANES:(c + 1) * LANES]


def _from_row_tiles(src_ref, rows):
    return jnp.concatenate([src_ref[pl.ds(c, rows, stride=ROW_SUB), :] for c in range(ROW_SUB)], axis=1)


def _tile_rows(r, n=1):
    return pl.ds(pl.multiple_of(r * ROW_SUB, ROW_SUB), n * ROW_SUB)


def _dispatch_kernel(zrow_ref, zflag_ref, pos_ref, h1_ref, xs_ref, src_ref, zeros_ref, sems, zsem, *, n_steps):
    tm = h1_ref.shape[0]
    i = pl.program_id(0)
    slot = i % 2

    @pl.when(i == 0)
    def _():
        zeros_ref[...] = jnp.zeros_like(zeros_ref)
        for e in range(2 * N_EXPERTS):
            @pl.when(zflag_ref[e] == 1)
            def _():
                cp = pltpu.make_async_copy(zeros_ref, xs_ref.at[_tile_rows(zrow_ref[e], MOE_T)], zsem)
                cp.start()
                cp.wait()

    def row_copy(s, t, k):
        return pltpu.make_async_copy(src_ref.at[s, _tile_rows(t)], xs_ref.at[_tile_rows(pos_ref[t * TOP_K + k])],
                                     sems.at[s])

    def wait_slot(s):
        _row_copy_loop(tm, lambda t, k: row_copy(s, t, k).wait())

    @pl.when(i >= 2)
    def _():
        wait_slot(slot)

    _to_row_tiles(src_ref.at[slot], h1_ref[...])
    _row_copy_loop(tm, lambda t, k: row_copy(slot, t, k).start())

    @pl.when(i == n_steps - 1)
    def _():
        if n_steps >= 2:
            wait_slot(1 - slot)
        wait_slot(slot)


def _dispatch(h1, pos_flat, zrow, zflag, n_rows_sorted):
    n = h1.shape[0]
    tm = min(ROW_TM, n)
    return pl.pallas_call(
        functools.partial(_dispatch_kernel, n_steps=n // tm),
        grid_spec=pltpu.PrefetchScalarGridSpec(
            num_scalar_prefetch=2,
            grid=(n // tm,),
            in_specs=[
                pl.BlockSpec((tm * TOP_K,), lambda i, zr, zf: (i,), memory_space=pltpu.SMEM),
                pl.BlockSpec((tm, D_MODEL), lambda i, zr, zf: (i, 0)),
            ],
            out_specs=pl.BlockSpec(memory_space=pl.ANY),
            scratch_shapes=[pltpu.VMEM((2, tm * ROW_SUB, LANES), f32), pltpu.VMEM((MOE_T * ROW_SUB, LANES), f32),
                            pltpu.SemaphoreType.DMA((2,)), pltpu.SemaphoreType.DMA],
        ),
        out_shape=jax.ShapeDtypeStruct((n_rows_sorted * ROW_SUB, LANES), f32),
        compiler_params=pltpu.CompilerParams(
            dimension_semantics=("arbitrary",), vmem_limit_bytes=VMEM_LIMIT_BYTES),
        name="moe_dispatch",
    )(zrow, zflag, pos_flat, h1)


def _ffn_kernel(te_ref, nt_ref, seg_ref, nxt_ref, x_ref, wgu_hbm, wd_hbm, bgu_ref, bd_ref, perm_ref, y_ref,
                wgu_f, wd_f, wgu_s, wd_s, sems):
    j = pl.program_id(0)

    def weight_copies(e, s):
        return (pltpu.make_async_copy(wgu_hbm.at[0, e], wgu_f.at[s], sems.at[s, 0]),
                pltpu.make_async_copy(wd_hbm.at[0, e], wd_f.at[s], sems.at[s, 1]))

    @pl.when(j < nt_ref[0])
    def _():
        slot = seg_ref[j] % 2

        @pl.when(j == 0)
        def _():
            for cp in weight_copies(te_ref[0], 0):
                cp.start()

        @pl.when(jnp.logical_or(j == 0, te_ref[j] != te_ref[jnp.maximum(j - 1, 0)]))
        def _():
            for cp in weight_copies(te_ref[j], slot):
                cp.wait()

            @pl.when(nxt_ref[j] >= 0)
            def _():
                for cp in weight_copies(nxt_ref[j], 1 - slot):
                    cp.start()

            for r0 in range(0, D_MODEL, LANES):
                wgu_s[r0:r0 + LANES, :] = wgu_f[slot, r0:r0 + LANES, :].astype(bf16)
            for r0 in range(0, D_FF, LANES):
                wd_s[r0:r0 + LANES, :] = _dot(perm_ref[...], wd_f[slot, r0:r0 + LANES, :].astype(bf16)).astype(bf16)

        x = _from_row_tiles(x_ref, MOE_T).astype(bf16)
        acc = jnp.broadcast_to(bd_ref[0], (MOE_T, D_MODEL))
        even = (lax.broadcasted_iota(i32, (MOE_T, LANES), 1) & 1) == 0
        for c0 in range(0, D_FF, FFN_FC):
            cols = slice(2 * c0, 2 * (c0 + FFN_FC))
            hgu = _dot(x, wgu_s[:, cols]) + bgu_ref[0, :, cols]
            zs = []
            for j0 in range(0, 2 * FFN_FC, 2 * LANES):
                lo = hgu[:, j0:j0 + LANES]
                hi = hgu[:, j0 + LANES:j0 + 2 * LANES]
                gate = jnp.where(even, lo, pltpu.roll(hi, 1, axis=1))
                up = jnp.where(even, pltpu.roll(lo, LANES - 1, axis=1), hi)
                gate = jnp.minimum(gate, SWIGLU_LIMIT)
                up = jnp.clip(up, -SWIGLU_LIMIT, SWIGLU_LIMIT)
                glu = gate * _sigmoid(gate * SWIGLU_ALPHA)
                zs.append(((up + 1.0) * glu).astype(bf16))
            acc = acc + _dot(jnp.concatenate(zs, axis=1), wd_s[c0:c0 + FFN_FC, :])
        _to_row_tiles(y_ref, acc)

    @pl.when(j >= nt_ref[0])
    def _():
        y_ref[...] = jnp.zeros_like(y_ref)


def _ffn(xs, tile_e, n_tiles, seg, nxt, wgu, wd, bgu, bd):
    p = xs.shape[0] // ROW_SUB
    last = lambda j, te, nt, sg, nx: jnp.minimum(j, nt[0] - 1)
    bmap = lambda j, te, nt, sg, nx: (te[j], 0, 0)
    r = np.arange(LANES)
    perm = np.zeros((LANES, LANES), np.float32)
    perm[r, (r % 2) * (LANES // 2) + r // 2] = 1.0
    return pl.pallas_call(
        _ffn_kernel,
        grid_spec=pltpu.PrefetchScalarGridSpec(
            num_scalar_prefetch=4,
            grid=(p // MOE_T,),
            in_specs=[
                pl.BlockSpec((MOE_T * ROW_SUB, LANES), lambda j, te, nt, sg, nx: (last(j, te, nt, sg, nx), 0)),
                pl.BlockSpec(memory_space=pl.ANY),
                pl.BlockSpec(memory_space=pl.ANY),
                pl.BlockSpec((1, 1, 2 * D_FF), bmap),
                pl.BlockSpec((1, 1, D_MODEL), bmap),
                pl.BlockSpec((LANES, LANES), lambda j, te, nt, sg, nx: (0, 0)),
            ],
            out_specs=pl.BlockSpec((MOE_T * ROW_SUB, LANES), lambda j, te, nt, sg, nx: (j, 0)),
            scratch_shapes=[pltpu.VMEM((2, D_MODEL, 2 * D_FF), f32), pltpu.VMEM((2, D_FF, D_MODEL), f32),
                            pltpu.VMEM((D_MODEL, 2 * D_FF), bf16), pltpu.VMEM((D_FF, D_MODEL), bf16),
                            pltpu.SemaphoreType.DMA((2, 2))],
        ),
        out_shape=jax.ShapeDtypeStruct((p * ROW_SUB, LANES), f32),
        compiler_params=pltpu.CompilerParams(
            dimension_semantics=("arbitrary",), vmem_limit_bytes=VMEM_LIMIT_BYTES),
        name="moe_ffn",
    )(tile_e, n_tiles, seg, nxt, xs, wgu, wd, bgu, bd, jnp.asarray(perm, dtype=bf16))


def _combine_kernel(pos_ref, nxt_ref, h1_ref, tw_ref, g2_ref, b2_ref, ys_ref, o_ref, ybuf_ref, sems, *, n_steps):
    tm = h1_ref.shape[0]
    i = pl.program_id(0)
    slot = i % 2

    def row_copy(p_ref, s, t, k):
        return pltpu.make_async_copy(ys_ref.at[_tile_rows(p_ref[t * TOP_K + k])], ybuf_ref.at[s, k, _tile_rows(t)],
                                     sems.at[s])

    @pl.when(i == 0)
    def _():
        _row_copy_loop(tm, lambda t, k: row_copy(pos_ref, 0, t, k).start())

    @pl.when(i + 1 < n_steps)
    def _():
        _row_copy_loop(tm, lambda t, k: row_copy(nxt_ref, 1 - slot, t, k).start())

    _row_copy_loop(tm, lambda t, k: row_copy(pos_ref, slot, t, k).wait())

    tw = tw_ref[...]
    ffn = tw[:, 0:1] * _from_row_tiles(ybuf_ref.at[slot, 0], tm)
    for k in range(1, TOP_K):
        ffn = ffn + tw[:, k:k + 1] * _from_row_tiles(ybuf_ref.at[slot, k], tm)
    o_ref[...] = _layer_norm(DN_ALPHA * h1_ref[...] + ffn, g2_ref[...], b2_ref[...])


def _combine(h1, tw, pos_flat, ys, g2, b2):
    n = h1.shape[0]
    tm = min(ROW_TM, n)
    n_steps = n // tm
    return pl.pallas_call(
        functools.partial(_combine_kernel, n_steps=n_steps),
        grid=(n_steps,),
        in_specs=[
            pl.BlockSpec((tm * TOP_K,), lambda i: (i,), memory_space=pltpu.SMEM),
            pl.BlockSpec((tm * TOP_K,), lambda i: (jnp.minimum(i + 1, n_steps - 1),), memory_space=pltpu.SMEM),
            pl.BlockSpec((tm, D_MODEL), lambda i: (i, 0)),
            pl.BlockSpec((tm, TOP_K), lambda i: (i, 0)),
            pl.BlockSpec(g2.shape, lambda i: (0, 0)), pl.BlockSpec(b2.shape, lambda i: (0, 0)),
            pl.BlockSpec(memory_space=pl.ANY),
        ],
        out_specs=pl.BlockSpec((tm, D_MODEL), lambda i: (i, 0)),
        out_shape=jax.ShapeDtypeStruct((n, D_MODEL), f32),
        scratch_shapes=[pltpu.VMEM((2, TOP_K, tm * ROW_SUB, LANES), f32), pltpu.SemaphoreType.DMA((2,))],
        compiler_params=pltpu.CompilerParams(
            dimension_semantics=("arbitrary",), vmem_limit_bytes=VMEM_LIMIT_BYTES),
        name="moe_combine_ln",
    )(pos_flat, pos_flat, h1, tw, g2, b2, ys)


def _routing_tables(counts, idx, rank, n_tiles_max):
    padded = ((counts + MOE_T - 1) // MOE_T) * MOE_T
    ends = jnp.cumsum(padded)
    offs = ends - padded
    pos = (offs[idx] + rank).reshape(-1)
    n_tiles = (ends[-1] // MOE_T).astype(i32)
    tile_start = jnp.minimum(jnp.arange(n_tiles_max, dtype=i32), n_tiles - 1) * MOE_T
    tile_e = jnp.minimum(jnp.sum((ends[None, :] <= tile_start[:, None]).astype(i32), axis=1), N_EXPERTS - 1)
    trailing = jnp.arange(n_tiles_max - N_EXPERTS, n_tiles_max, dtype=i32)
    zrow = jnp.concatenate([(ends - MOE_T).astype(i32), trailing * MOE_T])
    zflag = jnp.concatenate([(padded > counts).astype(i32), (trailing >= n_tiles).astype(i32)])
    seg = jnp.cumsum(jnp.concatenate([jnp.zeros((1,), i32), (tile_e[1:] != tile_e[:-1]).astype(i32)]))
    e_ids = jnp.arange(N_EXPERTS, dtype=i32)
    later = (e_ids[None, :] > e_ids[:, None]) & (counts[None, :] > 0)
    next_e = jnp.min(jnp.where(later, e_ids[None, :], N_EXPERTS), axis=1)
    nxt = jnp.where(next_e < N_EXPERTS, next_e, -1)[tile_e]
    return pos.astype(i32), tile_e, n_tiles.reshape(1), zrow, zflag, seg.astype(i32), nxt.astype(i32)


def kernel(x, w_in, hg_lb_logits, hg_norm_g, da_lambda, da_norm_g, w_branch_a, w_branch_b, w_out, ln1_g, ln1_b,
           router_w, router_b, w_gate_up, b_gate_up, w_down, b_down, ln2_g, ln2_b):
    batch, seq, d = x.shape
    assert d == D_MODEL and w_in.shape[0] == DEPTH == 1
    n = batch * seq
    x2d = x.reshape(n, d)

    w = w_in[0]
    o_hi, o_ga = 2 * HG_KEY, 2 * HG_KEY + 2 * HG_WIDTH + 2 * DA_QK + DA_WIDTH
    w_all = jnp.concatenate([w[:, 0:o_hi], w[:, o_ga:], w[:, o_hi:o_ga]], axis=1).astype(bf16)
    pf, pb = _inproj(x2d, w_all, o_hi)

    o_a = _hgrn(pf, pb, hg_lb_logits, hg_norm_g[0].reshape(1, HG_WIDTH), batch, seq)
    o_b = _attn(pb, da_lambda[0], da_norm_g[0], batch, seq)

    h1, idx, tw, rank, cnt = _merge(
        o_a, o_b, pb, x2d, w_branch_a[0].astype(bf16), w_branch_b[0].astype(bf16), w_out[0].astype(bf16),
        ln1_g[0].reshape(1, d), ln1_b[0].reshape(1, d), router_w[0].astype(bf16), router_b[0].reshape(1, N_EXPERTS))

    n_rows_sorted = n * TOP_K + N_EXPERTS * MOE_T
    pos, tile_e, n_tiles, zrow, zflag, seg, nxt = _routing_tables(
        cnt[0].astype(i32), idx, rank, n_rows_sorted // MOE_T)

    xs = _dispatch(h1, pos, zrow, zflag, n_rows_sorted)
    ys = _ffn(xs, tile_e, n_tiles, seg, nxt, w_gate_up, w_down, b_gate_up[0][:, None, :], b_down[0][:, None, :])
    out = _combine(h1, tw, pos, ys, ln2_g[0].reshape(1, d), ln2_b[0].reshape(1, d))
    return out.reshape(batch, seq, d)
```

```python
import functools
import math

import jax
import jax.numpy as jnp
import numpy as np
from jax import lax
from jax.experimental import pallas as pl
from jax.experimental.pallas import tpu as pltpu

f32 = jnp.float32
bf16 = jnp.bfloat16
i32 = jnp.int32

D_MODEL = 1024
DEPTH = 1
HG_HEADS = 4
HG_DK = 128
HG_DV = 128
HG_KEY = HG_HEADS * HG_DK
HG_WIDTH = HG_HEADS * HG_DV
DA_HEADS = 4
DA_HD = 64
DA_DV = 2 * DA_HD
DA_QK = DA_HEADS * 2 * DA_HD
DA_WIDTH = DA_HEADS * DA_DV
N_EXPERTS = 32
TOP_K = 4
D_FF = 1024
SWIGLU_LIMIT = 7.0
SWIGLU_ALPHA = 1.702
DN_ALPHA = (2.0 * DEPTH) ** 0.25
LN_EPS = 1e-5
NORM_EPS = 1e-6
LAMBDA_INIT = 0.8 - 0.6 * math.exp(-0.3 * 0)
LOG2_E = 1.0 / math.log(2.0)

VMEM_LIMIT_BYTES = 52 * 1024 * 1024

PROJ_TM = 2048
PROJ_TN = 512
HG_CHUNK = 256
HG_ROWS = 512
ATT_T = 256
ATT_R = 128
ATT_U = 4
MERGE_TM = 512
MOE_T = 256
ROW_TM = 256
FFN_FC = 1024
LANES = 128
ROW_SUB = D_MODEL // LANES

PB_GA = 0
PB_GB = PB_GA + D_MODEL
PB_HI = PB_GB + D_MODEL
PB_HG = PB_HI + HG_WIDTH
PB_DQ = PB_HG + HG_WIDTH
PB_DK = PB_DQ + DA_QK
PB_DV = PB_DK + DA_QK


def _sigmoid(x):
    return 0.5 * jnp.tanh(0.5 * x) + 0.5


def _dot(a, b):
    return jnp.dot(a, b, preferred_element_type=f32)


def _dot_nt(a, b):
    return lax.dot_general(a, b, (((1,), (1,)), ((), ())), preferred_element_type=f32)


def _dot_tn(a, b):
    return lax.dot_general(a, b, (((0,), (0,)), ((), ())), preferred_element_type=f32)


def _inproj_kernel(x_ref, w_ref, of_ref, ob_ref, xb_ref, *, n_f32_tiles):
    j = pl.program_id(1)

    @pl.when(j == 0)
    def _():
        xb_ref[...] = x_ref[...].astype(bf16)

    r = _dot(xb_ref[...], w_ref[...])

    @pl.when(j < n_f32_tiles)
    def _():
        of_ref[...] = r

    @pl.when(j >= n_f32_tiles)
    def _():
        ob_ref[...] = r.astype(bf16)


def _inproj(x2d, w, n_f32_cols):
    n, k = x2d.shape
    m = w.shape[1]
    tm = min(PROJ_TM, n)
    nf = n_f32_cols // PROJ_TN
    return pl.pallas_call(
        functools.partial(_inproj_kernel, n_f32_tiles=nf),
        grid=(n // tm, m // PROJ_TN),
        in_specs=[pl.BlockSpec((tm, k), lambda i, j: (i, 0)),
                  pl.BlockSpec((k, PROJ_TN), lambda i, j: (0, j))],
        out_specs=[pl.BlockSpec((tm, PROJ_TN), lambda i, j: (i, jnp.minimum(j, nf - 1))),
                   pl.BlockSpec((tm, PROJ_TN), lambda i, j: (i, jnp.maximum(j - nf, 0)))],
        out_shape=[jax.ShapeDtypeStruct((n, n_f32_cols), f32), jax.ShapeDtypeStruct((n, m - n_f32_cols), bf16)],
        scratch_shapes=[pltpu.VMEM((tm, k), bf16)],
        compiler_params=pltpu.CompilerParams(
            dimension_semantics=("parallel", "arbitrary"), vmem_limit_bytes=VMEM_LIMIT_BYTES),
        name="inproj",
    )(x2d, w)


def _shift_down(x, d):
    n = x.shape[0]
    if d % 8 == 0:
        return jnp.concatenate([x[n - d:], x[:n - d]], axis=0)
    return pltpu.roll(x, d, axis=0)


def _shift_up(x, d):
    n = x.shape[0]
    if d % 8 == 0:
        return jnp.concatenate([x[d:], x[:d]], axis=0)
    return pltpu.roll(x, n - d, axis=0)


def _hgrn_chunk_head(qv, fl, v, g, lb, ng, state, tril, lvl, row, eye):
    c = qv.shape[0]
    f = lb + (1.0 - lb) * _sigmoid(fl)
    logf = jnp.log(f)
    kk = 1.0 - f
    qa = qv * _sigmoid(qv)

    l1 = logf.astype(bf16)
    r1 = logf - l1.astype(f32)
    l2 = r1.astype(bf16)
    l3 = (r1 - l2.astype(f32)).astype(bf16)
    bc = _dot(tril, jnp.concatenate([l1, l2, l3], axis=1))
    b = bc[:, 0:HG_DK] + bc[:, HG_DK:2 * HG_DK] + bc[:, 2 * HG_DK:3 * HG_DK]

    blk = lvl.shape[0]
    nblk = c // blk
    assert nblk in (1, 2)
    qa_b, kk_b = qa.astype(bf16), kk.astype(bf16)
    diag = [jnp.where(lvl == -1, _dot_nt(qa_b[i * blk:(i + 1) * blk], kk_b[i * blk:(i + 1) * blk]), 0.0)
            for i in range(nblk)]
    off_diag = None
    filled = b
    d = 1
    level = 0
    while d < c:
        upper = (row & d) != 0
        ref_b = jnp.where(upper, _shift_down(filled, d), filled)
        e = jnp.exp2(jnp.abs(b - ref_b) * (-LOG2_E))
        mixed = (jnp.where(upper, qa, kk) * e).astype(bf16)
        if d < blk:
            for i in range(nblk):
                m_i = mixed[i * blk:(i + 1) * blk]
                diag[i] = jnp.where(lvl == level, _dot_nt(m_i, m_i), diag[i])
        else:
            off_diag = _dot_nt(mixed[d:2 * d], mixed[0:d])
        filled = jnp.where(upper, filled, _shift_up(filled, d))
        d *= 2
        level += 1
    if nblk == 1:
        attn = diag[0]
    else:
        attn = jnp.concatenate([jnp.concatenate([diag[0], jnp.zeros_like(diag[0])], axis=1),
                                jnp.concatenate([off_diag, diag[1]], axis=1)], axis=0)

    o_intra = _dot(attn.astype(bf16), v)
    o_inter = _dot((qa * jnp.exp(b)).astype(bf16), state.astype(bf16))

    b_last = b[c - 1:c, :]
    k_dec = (kk * jnp.exp(b_last - b)).astype(bf16)
    dcol = jnp.sum(jnp.where(eye, jnp.broadcast_to(jnp.exp(b_last), eye.shape), 0.0), axis=1, keepdims=True)
    new_state = dcol * state + _dot_tn(k_dec, v)

    o = o_inter + o_intra
    ms = jnp.mean(o * o, axis=1, keepdims=True)
    o = o * lax.rsqrt(ms + NORM_EPS) * ng
    o = o * _sigmoid(g.astype(f32))
    return o.astype(bf16), new_state


def _hgrn_kernel(q_ref, f_ref, i_ref, g_ref, lbl_ref, ng_ref, tril_ref, lvl_ref, o_ref, st_ref, *, chunk, n_chunks):
    @pl.when(pl.program_id(1) == 0)
    def _():
        st_ref[...] = jnp.zeros_like(st_ref)

    lbl = lbl_ref[...]
    ex = jnp.exp(lbl - jnp.max(lbl, axis=0, keepdims=True))
    lb_all = ex[0:1, :] / jnp.sum(ex, axis=0, keepdims=True)
    ng_all = ng_ref[...]
    tril = tril_ref[...]
    lvl = lvl_ref[...]
    row = lax.broadcasted_iota(i32, (chunk, HG_DK), 0)
    eye = lax.broadcasted_iota(i32, (HG_DK, HG_DV), 0) == lax.broadcasted_iota(i32, (HG_DK, HG_DV), 1)

    def body(ci, carry):
        r0 = pl.multiple_of(ci * chunk, chunk)
        for h in range(HG_HEADS):
            cs = pl.ds(h * HG_DK, HG_DK)
            out, new_state = _hgrn_chunk_head(
                q_ref[pl.ds(r0, chunk), cs], f_ref[pl.ds(r0, chunk), cs],
                i_ref[pl.ds(r0, chunk), cs], g_ref[pl.ds(r0, chunk), cs],
                lb_all[:, h * HG_DK:(h + 1) * HG_DK], ng_all[:, h * HG_DV:(h + 1) * HG_DV],
                st_ref[h], tril, lvl, row, eye)
            o_ref[pl.ds(r0, chunk), cs] = out
            st_ref[h] = new_state
        return carry

    lax.fori_loop(0, n_chunks, body, 0)


def _hgrn_consts(chunk):
    t = np.arange(chunk)
    tril = (t[None, :] <= t[:, None]).astype(np.float32)
    x = t[:, None] ^ t[None, :]
    lvl = np.where(x > 0, np.floor(np.log2(np.maximum(x, 1))).astype(np.int32), -1)
    lvl = np.where(t[:, None] >= t[None, :], lvl, -2).astype(np.int32)
    return jnp.asarray(tril, dtype=bf16), jnp.asarray(lvl, dtype=i32)


def _hgrn(pf, pb, lb_logits, norm_g, batch, seq):
    n = batch * seq
    rows = min(HG_ROWS, seq)
    chunk = min(HG_CHUNK, rows)
    spb = seq // rows
    tril, lvl = _hgrn_consts(chunk)
    lvl = lvl[:min(chunk, LANES), :min(chunk, LANES)]
    row_blk = lambda b, s: b * spb + s
    return pl.pallas_call(
        functools.partial(_hgrn_kernel, chunk=chunk, n_chunks=rows // chunk),
        grid=(batch, spb),
        in_specs=[
            pl.BlockSpec((rows, HG_KEY), lambda b, s: (row_blk(b, s), 0)),
            pl.BlockSpec((rows, HG_KEY), lambda b, s: (row_blk(b, s), 1)),
            pl.BlockSpec((rows, HG_WIDTH), lambda b, s: (row_blk(b, s), PB_HI // HG_WIDTH)),
            pl.BlockSpec((rows, HG_WIDTH), lambda b, s: (row_blk(b, s), PB_HG // HG_WIDTH)),
            pl.BlockSpec(lb_logits.shape, lambda b, s: (0, 0)),
            pl.BlockSpec((1, HG_WIDTH), lambda b, s: (0, 0)),
            pl.BlockSpec((chunk, chunk), lambda b, s: (0, 0)),
            pl.BlockSpec(lvl.shape, lambda b, s: (0, 0)),
        ],
        out_specs=pl.BlockSpec((rows, HG_WIDTH), lambda b, s: (row_blk(b, s), 0)),
        out_shape=jax.ShapeDtypeStruct((n, HG_WIDTH), bf16),
        scratch_shapes=[pltpu.VMEM((HG_HEADS, HG_DK, HG_DV), f32)],
        compiler_params=pltpu.CompilerParams(
            dimension_semantics=("parallel", "arbitrary"), vmem_limit_bytes=VMEM_LIMIT_BYTES),
        name="hgrn2",
    )(pf, pf, pb, pb, lb_logits, norm_g, tril, lvl)


def _attn_kernel(lam_ref, ng_ref, q_ref, k_ref, v_ref, o_ref, qs_ref, m_ref, acc_ref, *, t):
    qi = pl.program_id(1)
    blk = 2 * DA_HD

    lane = lax.broadcasted_iota(i32, (t, blk), 1)
    for h in range(DA_HEADS):
        q = q_ref[:, h * blk:(h + 1) * blk] * jnp.asarray(DA_HD ** -0.5, bf16)
        zero = jnp.zeros_like(q)
        qs_ref[h, 0:t, :] = jnp.where(lane < DA_HD, q, zero)
        qs_ref[h, t:2 * t, :] = jnp.where(lane >= DA_HD, q, zero)
    m_ref[...] = jnp.full_like(m_ref, -jnp.inf)
    acc_ref[...] = jnp.zeros_like(acc_ref)

    col = lax.broadcasted_iota(i32, (1, t), 1)
    ones = jnp.ones((t, DA_DV), bf16)

    def step(kt, masked):
        k0 = pl.multiple_of(kt * t, t)
        rel = ((kt - qi) * t + col).astype(f32)
        if masked:
            rr = lax.broadcasted_iota(i32, (2 * t, t), 0)
            rr = jnp.where(rr >= t, rr - t, rr)
            causal = lax.broadcasted_iota(i32, (2 * t, t), 1) <= rr
        for h in range(DA_HEADS):
            cs = pl.ds(h * blk, blk)
            k = k_ref[pl.ds(k0, t), cs]
            vo = jnp.concatenate([v_ref[pl.ds(k0, t), cs], ones], axis=1)
            bias = (2.0 ** (-8.0 * (h + 1) / DA_HEADS)) * rel
            for r0 in range(0, 2 * t, ATT_R):
                rows = slice(r0, r0 + ATT_R)
                s = _dot_nt(qs_ref[h, rows, :], k) + bias
                if masked:
                    s = jnp.where(causal[rows, :], s, -jnp.inf)
                m_old = m_ref[h, rows, :]
                m_new = jnp.maximum(m_old, jnp.max(s, axis=1, keepdims=True))
                alpha = jnp.exp(m_old - m_new)
                p = jnp.exp(s - jnp.concatenate([m_new] * (t // DA_DV), axis=1)).astype(bf16)
                acc_ref[h, rows, :] = jnp.concatenate([alpha, alpha], axis=1) * acc_ref[h, rows, :] + _dot(p, vo)
                m_ref[h, rows, :] = m_new

    def loop_body(kp, carry):
        for u in range(ATT_U):
            step(ATT_U * kp + u, False)
        return carry

    lax.fori_loop(0, qi // ATT_U, loop_body, 0)

    def tail_body(kt, carry):
        step(kt, False)
        return carry

    lax.fori_loop((qi // ATT_U) * ATT_U, qi, tail_body, 0)

    step(qi, True)

    lp = lam_ref[...]
    lam = (jnp.exp(jnp.sum(lp[0:1, :] * lp[1:2, :], axis=1, keepdims=True))
           - jnp.exp(jnp.sum(lp[2:3, :] * lp[3:4, :], axis=1, keepdims=True)) + LAMBDA_INIT)
    for h in range(DA_HEADS):
        acc = acc_ref[h]
        o_all = acc[:, 0:DA_DV] / acc[:, DA_DV:2 * DA_DV]
        o = o_all[0:t, :] - lam * o_all[t:2 * t, :]
        ms = jnp.mean(o * o, axis=1, keepdims=True)
        o = o * lax.rsqrt(ms + NORM_EPS) * ng_ref[:, h * DA_DV:(h + 1) * DA_DV] * (1.0 - LAMBDA_INIT)
        o_ref[:, h * DA_DV:(h + 1) * DA_DV] = o.astype(bf16)


def _attn(pb, lam_params, norm_g, batch, seq):
    n = batch * seq
    t = min(ATT_T, seq)
    nq = seq // t
    q0, k0, v0 = PB_DQ // DA_QK, PB_DK // DA_QK, PB_DV // DA_WIDTH
    return pl.pallas_call(
        functools.partial(_attn_kernel, t=t),
        grid=(batch, nq),
        in_specs=[
            pl.BlockSpec(lam_params.shape, lambda b, i: (0, 0)),
            pl.BlockSpec((1, DA_WIDTH), lambda b, i: (0, 0)),
            pl.BlockSpec((t, DA_QK), lambda b, i: (b * nq + i, q0)),
            pl.BlockSpec((seq, DA_QK), lambda b, i: (b, k0)),
            pl.BlockSpec((seq, DA_WIDTH), lambda b, i: (b, v0)),
        ],
        out_specs=pl.BlockSpec((t, DA_WIDTH), lambda b, i: (b * nq + i, 0)),
        out_shape=jax.ShapeDtypeStruct((n, DA_WIDTH), bf16),
        scratch_shapes=[pltpu.VMEM((DA_HEADS, 2 * t, 2 * DA_HD), bf16), pltpu.VMEM((DA_HEADS, 2 * t, DA_DV), f32),
                        pltpu.VMEM((DA_HEADS, 2 * t, 2 * DA_DV), f32)],
        compiler_params=pltpu.CompilerParams(
            dimension_semantics=("parallel", "arbitrary"), vmem_limit_bytes=VMEM_LIMIT_BYTES),
        name="diff_attn",
    )(lam_params, norm_g.reshape(1, DA_WIDTH), pb, pb, pb)


def _layer_norm(y, g, b):
    mu = jnp.mean(y, axis=1, keepdims=True)
    yc = y - mu
    var = jnp.mean(yc * yc, axis=1, keepdims=True)
    return yc * lax.rsqrt(var + LN_EPS) * g + b


def _assemble4(cols, dtype):
    tm = cols[0].shape[0]
    lane = lax.broadcasted_iota(i32, (tm, TOP_K), 1)
    out = jnp.broadcast_to(cols[TOP_K - 1], (tm, TOP_K))
    for k in range(TOP_K - 2, -1, -1):
        out = jnp.where(lane == k, jnp.broadcast_to(cols[k], (tm, TOP_K)), out)
    return out.astype(dtype)


def _merge_kernel(oa_ref, ob_ref, ga_ref, gb_ref, x_ref, wa_ref, wb_ref, wo_ref, g1_ref, b1_ref, rw_ref, rb_ref,
                  tri_ref, h1_ref, idx_ref, tw_ref, rank_ref, cnt_ref, carry_ref):
    @pl.when(pl.program_id(0) == 0)
    def _():
        carry_ref[...] = jnp.zeros_like(carry_ref)

    a = _dot(oa_ref[...], wa_ref[...])
    b = _dot(ob_ref[...], wb_ref[...])
    merged = _sigmoid(ga_ref[...].astype(f32)) * a + _sigmoid(gb_ref[...].astype(f32)) * b
    mix = _dot(merged.astype(bf16), wo_ref[...])
    h1 = _layer_norm(DN_ALPHA * x_ref[...] + mix, g1_ref[...], b1_ref[...])
    h1_ref[...] = h1

    logits = _dot(h1.astype(bf16), rw_ref[...]) + rb_ref[...]
    tm = logits.shape[0]
    lane = lax.broadcasted_iota(i32, (tm, N_EXPERTS), 1).astype(f32)
    work = logits
    vals, idxs = [], []
    for _ in range(TOP_K):
        mk = jnp.max(work, axis=1, keepdims=True)
        ik = jnp.min(jnp.where(work == mk, lane, float(N_EXPERTS)), axis=1, keepdims=True)
        vals.append(mk)
        idxs.append(ik)
        work = jnp.where(lane == ik, -jnp.inf, work)
    es = [jnp.exp(v - vals[0]) for v in vals]
    den = es[0] + es[1] + es[2] + es[3]
    tw_ref[...] = _assemble4([e / den for e in es], f32)
    idx_ref[...] = _assemble4(idxs, i32)

    onehot = jnp.zeros((tm, N_EXPERTS), f32)
    for ik in idxs:
        onehot = onehot + jnp.where(lane == ik, 1.0, 0.0)
    before = _dot(tri_ref[...], onehot.astype(bf16)) + carry_ref[...]
    ranks = [jnp.sum(jnp.where(lane == ik, before, 0.0), axis=1, keepdims=True) for ik in idxs]
    rank_ref[...] = _assemble4(ranks, i32)
    total = carry_ref[...] + jnp.sum(onehot, axis=0, keepdims=True)
    carry_ref[...] = total
    cnt_ref[...] = total


def _merge(o_a, o_b, pb, x2d, wa, wb, wo, g1, b1, rw, rb):
    n = x2d.shape[0]
    tm = min(MERGE_TM, n)
    t = np.arange(tm)
    tri = jnp.asarray((t[None, :] < t[:, None]).astype(np.float32), dtype=bf16)
    row = lambda i: (i, 0)
    const = lambda i: (0, 0)
    return pl.pallas_call(
        _merge_kernel,
        grid=(n // tm,),
        in_specs=[
            pl.BlockSpec((tm, HG_WIDTH), row),
            pl.BlockSpec((tm, DA_WIDTH), row),
            pl.BlockSpec((tm, D_MODEL), lambda i: (i, PB_GA // D_MODEL)),
            pl.BlockSpec((tm, D_MODEL), lambda i: (i, PB_GB // D_MODEL)),
            pl.BlockSpec((tm, D_MODEL), row),
            pl.BlockSpec(wa.shape, const), pl.BlockSpec(wb.shape, const), pl.BlockSpec(wo.shape, const),
            pl.BlockSpec(g1.shape, const), pl.BlockSpec(b1.shape, const),
            pl.BlockSpec(rw.shape, const), pl.BlockSpec(rb.shape, const),
            pl.BlockSpec((tm, tm), const),
        ],
        out_specs=[
            pl.BlockSpec((tm, D_MODEL), row),
            pl.BlockSpec((tm, TOP_K), row), pl.BlockSpec((tm, TOP_K), row), pl.BlockSpec((tm, TOP_K), row),
            pl.BlockSpec((1, N_EXPERTS), const),
        ],
        out_shape=[
            jax.ShapeDtypeStruct((n, D_MODEL), f32),
            jax.ShapeDtypeStruct((n, TOP_K), i32), jax.ShapeDtypeStruct((n, TOP_K), f32),
            jax.ShapeDtypeStruct((n, TOP_K), i32),
            jax.ShapeDtypeStruct((1, N_EXPERTS), f32),
        ],
        scratch_shapes=[pltpu.VMEM((1, N_EXPERTS), f32)],
        compiler_params=pltpu.CompilerParams(
            dimension_semantics=("arbitrary",), vmem_limit_bytes=VMEM_LIMIT_BYTES),
        name="merge_ln_router",
    )(o_a, o_b, pb, pb, x2d, wa, wb, wo, g1, b1, rw, rb, tri)


def _row_copy_loop(n_rows, start_one):
    def body(t, carry):
        for k in range(TOP_K):
            start_one(t, k)
        return carry
    lax.fori_loop(0, n_rows, body, 0)


def _to_row_tiles(dst_ref, x):
    rows = x.shape[0]
    for c in range(ROW_SUB):
        dst_ref[pl.ds(c, rows, stride=ROW_SUB), :] = x[:, c * LANES:(c + 1) * LANES]


def _from_row_tiles(src_ref, rows):
    return jnp.concatenate([src_ref[pl.ds(c, rows, stride=ROW_SUB), :] for c in range(ROW_SUB)], axis=1)


def _tile_rows(r, n=1):
    return pl.ds(pl.multiple_of(r * ROW_SUB, ROW_SUB), n * ROW_SUB)


def _dispatch_kernel(zrow_ref, zflag_ref, pos_ref, h1_ref, xs_ref, src_ref, zeros_ref, sems, zsem, *, n_steps):
    tm = h1_ref.shape[0]
    i = pl.program_id(0)
    slot = i % 2

    @pl.when(i == 0)
    def _():
        zeros_ref[...] = jnp.zeros_like(zeros_ref)
        for e in range(2 * N_EXPERTS):
            @pl.when(zflag_ref[e] == 1)
            def _():
                cp = pltpu.make_async_copy(zeros_ref, xs_ref.at[_tile_rows(zrow_ref[e], MOE_T)], zsem)
                cp.start()
                cp.wait()

    def row_copy(s, t, k):
        return pltpu.make_async_copy(src_ref.at[s, _tile_rows(t)], xs_ref.at[_tile_rows(pos_ref[t * TOP_K + k])],
                                     sems.at[s])

    def wait_slot(s):
        _row_copy_loop(tm, lambda t, k: row_copy(s, t, k).wait())

    @pl.when(i >= 2)
    def _():
        wait_slot(slot)

    _to_row_tiles(src_ref.at[slot], h1_ref[...])
    _row_copy_loop(tm, lambda t, k: row_copy(slot, t, k).start())

    @pl.when(i == n_steps - 1)
    def _():
        if n_steps >= 2:
            wait_slot(1 - slot)
        wait_slot(slot)


def _dispatch(h1, pos_flat, zrow, zflag, n_rows_sorted):
    n = h1.shape[0]
    tm = min(ROW_TM, n)
    return pl.pallas_call(
        functools.partial(_dispatch_kernel, n_steps=n // tm),
        grid_spec=pltpu.PrefetchScalarGridSpec(
            num_scalar_prefetch=2,
            grid=(n // tm,),
            in_specs=[
                pl.BlockSpec((tm * TOP_K,), lambda i, zr, zf: (i,), memory_space=pltpu.SMEM),
                pl.BlockSpec((tm, D_MODEL), lambda i, zr, zf: (i, 0)),
            ],
            out_specs=pl.BlockSpec(memory_space=pl.ANY),
            scratch_shapes=[pltpu.VMEM((2, tm * ROW_SUB, LANES), f32), pltpu.VMEM((MOE_T * ROW_SUB, LANES), f32),
                            pltpu.SemaphoreType.DMA((2,)), pltpu.SemaphoreType.DMA],
        ),
        out_shape=jax.ShapeDtypeStruct((n_rows_sorted * ROW_SUB, LANES), f32),
        compiler_params=pltpu.CompilerParams(
            dimension_semantics=("arbitrary",), vmem_limit_bytes=VMEM_LIMIT_BYTES),
        name="moe_dispatch",
    )(zrow, zflag, pos_flat, h1)


def _ffn_kernel(te_ref, nt_ref, seg_ref, nxt_ref, x_ref, wgu_hbm, wd_hbm, bgu_ref, bd_ref, perm_ref, y_ref,
                wgu_f, wd_f, wgu_s, wd_s, sems):
    j = pl.program_id(0)

    def weight_copies(e, s):
        return (pltpu.make_async_copy(wgu_hbm.at[0, e], wgu_f.at[s], sems.at[s, 0]),
                pltpu.make_async_copy(wd_hbm.at[0, e], wd_f.at[s], sems.at[s, 1]))

    @pl.when(j < nt_ref[0])
    def _():
        slot = seg_ref[j] % 2

        @pl.when(j == 0)
        def _():
            for cp in weight_copies(te_ref[0], 0):
                cp.start()

        @pl.when(jnp.logical_or(j == 0, te_ref[j] != te_ref[jnp.maximum(j - 1, 0)]))
        def _():
            for cp in weight_copies(te_ref[j], slot):
                cp.wait()

            @pl.when(nxt_ref[j] >= 0)
            def _():
                for cp in weight_copies(nxt_ref[j], 1 - slot):
                    cp.start()

            for r0 in range(0, D_MODEL, LANES):
                wgu_s[r0:r0 + LANES, :] = wgu_f[slot, r0:r0 + LANES, :].astype(bf16)
            for r0 in range(0, D_FF, LANES):
                wd_s[r0:r0 + LANES, :] = _dot(perm_ref[...], wd_f[slot, r0:r0 + LANES, :].astype(bf16)).astype(bf16)

        x = _from_row_tiles(x_ref, MOE_T).astype(bf16)
        acc = jnp.broadcast_to(bd_ref[0], (MOE_T, D_MODEL))
        even = (lax.broadcasted_iota(i32, (MOE_T, LANES), 1) & 1) == 0
        for c0 in range(0, D_FF, FFN_FC):
            cols = slice(2 * c0, 2 * (c0 + FFN_FC))
            hgu = _dot(x, wgu_s[:, cols]) + bgu_ref[0, :, cols]
            zs = []
            for j0 in range(0, 2 * FFN_FC, 2 * LANES):
                lo = hgu[:, j0:j0 + LANES]
                hi = hgu[:, j0 + LANES:j0 + 2 * LANES]
                gate = jnp.where(even, lo, pltpu.roll(hi, 1, axis=1))
                up = jnp.where(even, pltpu.roll(lo, LANES - 1, axis=1), hi)
                gate = jnp.minimum(gate, SWIGLU_LIMIT)
                up = jnp.clip(up, -SWIGLU_LIMIT, SWIGLU_LIMIT)
                glu = gate * _sigmoid(gate * SWIGLU_ALPHA)
                zs.append(((up + 1.0) * glu).astype(bf16))
            acc = acc + _dot(jnp.concatenate(zs, axis=1), wd_s[c0:c0 + FFN_FC, :])
        _to_row_tiles(y_ref, acc)

    @pl.when(j >= nt_ref[0])
    def _():
        y_ref[...] = jnp.zeros_like(y_ref)


def _ffn(xs, tile_e, n_tiles, seg, nxt, wgu, wd, bgu, bd):
    p = xs.shape[0] // ROW_SUB
    last = lambda j, te, nt, sg, nx: jnp.minimum(j, nt[0] - 1)
    bmap = lambda j, te, nt, sg, nx: (te[j], 0, 0)
    r = np.arange(LANES)
    perm = np.zeros((LANES, LANES), np.float32)
    perm[r, (r % 2) * (LANES // 2) + r // 2] = 1.0
    return pl.pallas_call(
        _ffn_kernel,
        grid_spec=pltpu.PrefetchScalarGridSpec(
            num_scalar_prefetch=4,
            grid=(p // MOE_T,),
            in_specs=[
                pl.BlockSpec((MOE_T * ROW_SUB, LANES), lambda j, te, nt, sg, nx: (last(j, te, nt, sg, nx), 0)),
                pl.BlockSpec(memory_space=pl.ANY),
                pl.BlockSpec(memory_space=pl.ANY),
                pl.BlockSpec((1, 1, 2 * D_FF), bmap),
                pl.BlockSpec((1, 1, D_MODEL), bmap),
                pl.BlockSpec((LANES, LANES), lambda j, te, nt, sg, nx: (0, 0)),
            ],
            out_specs=pl.BlockSpec((MOE_T * ROW_SUB, LANES), lambda j, te, nt, sg, nx: (j, 0)),
            scratch_shapes=[pltpu.VMEM((2, D_MODEL, 2 * D_FF), f32), pltpu.VMEM((2, D_FF, D_MODEL), f32),
                            pltpu.VMEM((D_MODEL, 2 * D_FF), bf16), pltpu.VMEM((D_FF, D_MODEL), bf16),
                            pltpu.SemaphoreType.DMA((2, 2))],
        ),
        out_shape=jax.ShapeDtypeStruct((p * ROW_SUB, LANES), f32),
        compiler_params=pltpu.CompilerParams(
            dimension_semantics=("arbitrary",), vmem_limit_bytes=VMEM_LIMIT_BYTES),
        name="moe_ffn",
    )(tile_e, n_tiles, seg, nxt, xs, wgu, wd, bgu, bd, jnp.asarray(perm, dtype=bf16))


def _combine_kernel(pos_ref, nxt_ref, h1_ref, tw_ref, g2_ref, b2_ref, ys_ref, o_ref, ybuf_ref, sems, *, n_steps):
    tm = h1_ref.shape[0]
    i = pl.program_id(0)
    slot = i % 2

    def row_copy(p_ref, s, t, k):
        return pltpu.make_async_copy(ys_ref.at[_tile_rows(p_ref[t * TOP_K + k])], ybuf_ref.at[s, k, _tile_rows(t)],
                                     sems.at[s])

    @pl.when(i == 0)
    def _():
        _row_copy_loop(tm, lambda t, k: row_copy(pos_ref, 0, t, k).start())

    @pl.when(i + 1 < n_steps)
    def _():
        _row_copy_loop(tm, lambda t, k: row_copy(nxt_ref, 1 - slot, t, k).start())

    _row_copy_loop(tm, lambda t, k: row_copy(pos_ref, slot, t, k).wait())

    tw = tw_ref[...]
    ffn = tw[:, 0:1] * _from_row_tiles(ybuf_ref.at[slot, 0], tm)
    for k in range(1, TOP_K):
        ffn = ffn + tw[:, k:k + 1] * _from_row_tiles(ybuf_ref.at[slot, k], tm)
    o_ref[...] = _layer_norm(DN_ALPHA * h1_ref[...] + ffn, g2_ref[...], b2_ref[...])


def _combine(h1, tw, pos_flat, ys, g2, b2):
    n = h1.shape[0]
    tm = min(ROW_TM, n)
    n_steps = n // tm
    return pl.pallas_call(
        functools.partial(_combine_kernel, n_steps=n_steps),
        grid=(n_steps,),
        in_specs=[
            pl.BlockSpec((tm * TOP_K,), lambda i: (i,), memory_space=pltpu.SMEM),
            pl.BlockSpec((tm * TOP_K,), lambda i: (jnp.minimum(i + 1, n_steps - 1),), memory_space=pltpu.SMEM),
            pl.BlockSpec((tm, D_MODEL), lambda i: (i, 0)),
            pl.BlockSpec((tm, TOP_K), lambda i: (i, 0)),
            pl.BlockSpec(g2.shape, lambda i: (0, 0)), pl.BlockSpec(b2.shape, lambda i: (0, 0)),
            pl.BlockSpec(memory_space=pl.ANY),
        ],
        out_specs=pl.BlockSpec((tm, D_MODEL), lambda i: (i, 0)),
        out_shape=jax.ShapeDtypeStruct((n, D_MODEL), f32),
        scratch_shapes=[pltpu.VMEM((2, TOP_K, tm * ROW_SUB, LANES), f32), pltpu.SemaphoreType.DMA((2,))],
        compiler_params=pltpu.CompilerParams(
            dimension_semantics=("arbitrary",), vmem_limit_bytes=VMEM_LIMIT_BYTES),
        name="moe_combine_ln",
    )(pos_flat, pos_flat, h1, tw, g2, b2, ys)


def _routing_tables(counts, idx, rank, n_tiles_max):
    padded = ((counts + MOE_T - 1) // MOE_T) * MOE_T
    ends = jnp.cumsum(padded)
    offs = ends - padded
    pos = (offs[idx] + rank).reshape(-1)
    n_tiles = (ends[-1] // MOE_T).astype(i32)
    tile_start = jnp.minimum(jnp.arange(n_tiles_max, dtype=i32), n_tiles - 1) * MOE_T
    tile_e = jnp.minimum(jnp.sum((ends[None, :] <= tile_start[:, None]).astype(i32), axis=1), N_EXPERTS - 1)
    trailing = jnp.arange(n_tiles_max - N_EXPERTS, n_tiles_max, dtype=i32)
    zrow = jnp.concatenate([(ends - MOE_T).astype(i32), trailing * MOE_T])
    zflag = jnp.concatenate([(padded > counts).astype(i32), (trailing >= n_tiles).astype(i32)])
    seg = jnp.cumsum(jnp.concatenate([jnp.zeros((1,), i32), (tile_e[1:] != tile_e[:-1]).astype(i32)]))
    e_ids = jnp.arange(N_EXPERTS, dtype=i32)
    later = (e_ids[None, :] > e_ids[:, None]) & (counts[None, :] > 0)
    next_e = jnp.min(jnp.where(later, e_ids[None, :], N_EXPERTS), axis=1)
    nxt = jnp.where(next_e < N_EXPERTS, next_e, -1)[tile_e]
    return pos.astype(i32), tile_e, n_tiles.reshape(1), zrow, zflag, seg.astype(i32), nxt.astype(i32)


def kernel(x, w_in, hg_lb_logits, hg_norm_g, da_lambda, da_norm_g, w_branch_a, w_branch_b, w_out, ln1_g, ln1_b,
           router_w, router_b, w_gate_up, b_gate_up, w_down, b_down, ln2_g, ln2_b):
    batch, seq, d = x.shape
    assert d == D_MODEL and w_in.shape[0] == DEPTH == 1
    n = batch * seq
    x2d = x.reshape(n, d)

    w = w_in[0]
    o_hi, o_ga = 2 * HG_KEY, 2 * HG_KEY + 2 * HG_WIDTH + 2 * DA_QK + DA_WIDTH
    w_all = jnp.concatenate([w[:, 0:o_hi], w[:, o_ga:], w[:, o_hi:o_ga]], axis=1).astype(bf16)
    pf, pb = _inproj(x2d, w_all, o_hi)

    o_a = _hgrn(pf, pb, hg_lb_logits, hg_norm_g[0].reshape(1, HG_WIDTH), batch, seq)
    o_b = _attn(pb, da_lambda[0], da_norm_g[0], batch, seq)

    h1, idx, tw, rank, cnt = _merge(
        o_a, o_b, pb, x2d, w_branch_a[0].astype(bf16), w_branch_b[0].astype(bf16), w_out[0].astype(bf16),
        ln1_g[0].reshape(1, d), ln1_b[0].reshape(1, d), router_w[0].astype(bf16), router_b[0].reshape(1, N_EXPERTS))

    n_rows_sorted = n * TOP_K + N_EXPERTS * MOE_T
    pos, tile_e, n_tiles, zrow, zflag, seg, nxt = _routing_tables(
        cnt[0].astype(i32), idx, rank, n_rows_sorted // MOE_T)

    xs = _dispatch(h1, pos, zrow, zflag, n_rows_sorted)
    ys = _ffn(xs, tile_e, n_tiles, seg, nxt, w_gate_up, w_down, b_gate_up[0][:, None, :], b_down[0][:, None, :])
    out = _combine(h1, tw, pos, ys, ln2_g[0].reshape(1, d), ln2_b[0].reshape(1, d))
    return out.reshape(batch, seq, d)
```

```python
import functools
import math

import jax
import jax.numpy as jnp
import numpy as np
from jax import lax
from jax.experimental import pallas as pl
from jax.experimental.pallas import tpu as pltpu

f32 = jnp.float32
bf16 = jnp.bfloat16
i32 = jnp.int32

D_MODEL = 1024
DEPTH = 1
HG_HEADS = 4
HG_DK = 128
HG_DV = 128
HG_KEY = HG_HEADS * HG_DK
HG_WIDTH = HG_HEADS * HG_DV
DA_HEADS = 4
DA_HD = 64
DA_DV = 2 * DA_HD
DA_QK = DA_HEADS * 2 * DA_HD
DA_WIDTH = DA_HEADS * DA_DV
N_EXPERTS = 32
TOP_K = 4
D_FF = 1024
SWIGLU_LIMIT = 7.0
SWIGLU_ALPHA = 1.702
DN_ALPHA = (2.0 * DEPTH) ** 0.25
LN_EPS = 1e-5
NORM_EPS = 1e-6
LAMBDA_INIT = 0.8 - 0.6 * math.exp(-0.3 * 0)
LOG2_E = 1.0 / math.log(2.0)

VMEM_LIMIT_BYTES = 52 * 1024 * 1024

PROJ_TM = 2048
PROJ_TN = 512
HG_CHUNK = 256
HG_ROWS = 512
ATT_T = 256
ATT_R = 128
ATT_U = 4
MERGE_TM = 1024
MOE_T = 256
ROW_TM = 256
FFN_FC = 1024
LANES = 128
RANK_BITS = 17
ROW_SUB = D_MODEL // LANES

PB_GA = 0
PB_GB = PB_GA + D_MODEL
PB_HI = PB_GB + D_MODEL
PB_HG = PB_HI + HG_WIDTH
PB_DQ = PB_HG + HG_WIDTH
PB_DK = PB_DQ + DA_QK
PB_DV = PB_DK + DA_QK


def _sigmoid(x):
    return 0.5 * jnp.tanh(0.5 * x) + 0.5


def _dot(a, b):
    return jnp.dot(a, b, preferred_element_type=f32)


def _dot_nt(a, b):
    return lax.dot_general(a, b, (((1,), (1,)), ((), ())), preferred_element_type=f32)


def _dot_tn(a, b):
    return lax.dot_general(a, b, (((0,), (0,)), ((), ())), preferred_element_type=f32)


def _inproj_kernel(x_ref, w_ref, of_ref, ob_ref, xb_ref, *, n_f32_tiles):
    j = pl.program_id(1)

    @pl.when(j == 0)
    def _():
        xb_ref[...] = x_ref[...].astype(bf16)

    r = _dot(xb_ref[...], w_ref[...])

    @pl.when(j < n_f32_tiles)
    def _():
        of_ref[...] = r

    @pl.when(j >= n_f32_tiles)
    def _():
        ob_ref[...] = r.astype(bf16)


def _inproj(x2d, w, n_f32_cols):
    n, k = x2d.shape
    m = w.shape[1]
    tm = min(PROJ_TM, n)
    nf = n_f32_cols // PROJ_TN
    return pl.pallas_call(
        functools.partial(_inproj_kernel, n_f32_tiles=nf),
        grid=(n // tm, m // PROJ_TN),
        in_specs=[pl.BlockSpec((tm, k), lambda i, j: (i, 0)),
                  pl.BlockSpec((k, PROJ_TN), lambda i, j: (0, j))],
        out_specs=[pl.BlockSpec((tm, PROJ_TN), lambda i, j: (i, jnp.minimum(j, nf - 1))),
                   pl.BlockSpec((tm, PROJ_TN), lambda i, j: (i, jnp.maximum(j - nf, 0)))],
        out_shape=[jax.ShapeDtypeStruct((n, n_f32_cols), f32), jax.ShapeDtypeStruct((n, m - n_f32_cols), bf16)],
        scratch_shapes=[pltpu.VMEM((tm, k), bf16)],
        compiler_params=pltpu.CompilerParams(
            dimension_semantics=("parallel", "arbitrary"), vmem_limit_bytes=VMEM_LIMIT_BYTES),
        name="inproj",
    )(x2d, w)


def _shift_down(x, d):
    n = x.shape[0]
    if d % 8 == 0:
        return jnp.concatenate([x[n - d:], x[:n - d]], axis=0)
    return pltpu.roll(x, d, axis=0)


def _shift_up(x, d):
    n = x.shape[0]
    if d % 8 == 0:
        return jnp.concatenate([x[d:], x[:d]], axis=0)
    return pltpu.roll(x, n - d, axis=0)


def _hgrn_chunk_head(qv, fl, v, g, lb, ng, state, tril, lvl, row, eye):
    c = qv.shape[0]
    f = lb + (1.0 - lb) * _sigmoid(fl)
    logf = jnp.log(f)
    kk = 1.0 - f
    qa = qv * _sigmoid(qv)

    l1 = logf.astype(bf16)
    r1 = logf - l1.astype(f32)
    l2 = r1.astype(bf16)
    l3 = (r1 - l2.astype(f32)).astype(bf16)
    bc = _dot(tril, jnp.concatenate([l1, l2, l3], axis=1))
    b = bc[:, 0:HG_DK] + bc[:, HG_DK:2 * HG_DK] + bc[:, 2 * HG_DK:3 * HG_DK]

    blk = lvl.shape[0]
    nblk = c // blk
    assert nblk in (1, 2)
    qa_b, kk_b = qa.astype(bf16), kk.astype(bf16)
    diag = [jnp.where(lvl == -1, _dot_nt(qa_b[i * blk:(i + 1) * blk], kk_b[i * blk:(i + 1) * blk]), 0.0)
            for i in range(nblk)]
    off_diag = None
    filled = b
    d = 1
    level = 0
    while d < c:
        upper = (row & d) != 0
        ref_b = jnp.where(upper, _shift_down(filled, d), filled)
        e = jnp.exp2(jnp.abs(b - ref_b) * (-LOG2_E))
        mixed = (jnp.where(upper, qa, kk) * e).astype(bf16)
        if d < blk:
            for i in range(nblk):
                m_i = mixed[i * blk:(i + 1) * blk]
                diag[i] = jnp.where(lvl == level, _dot_nt(m_i, m_i), diag[i])
        else:
            off_diag = _dot_nt(mixed[d:2 * d], mixed[0:d])
        filled = jnp.where(upper, filled, _shift_up(filled, d))
        d *= 2
        level += 1
    if nblk == 1:
        attn = diag[0]
    else:
        attn = jnp.concatenate([jnp.concatenate([diag[0], jnp.zeros_like(diag[0])], axis=1),
                                jnp.concatenate([off_diag, diag[1]], axis=1)], axis=0)

    o_intra = _dot(attn.astype(bf16), v)
    o_inter = _dot((qa * jnp.exp(b)).astype(bf16), state.astype(bf16))

    b_last = b[c - 1:c, :]
    k_dec = (kk * jnp.exp(b_last - b)).astype(bf16)
    dcol = jnp.sum(jnp.where(eye, jnp.broadcast_to(jnp.exp(b_last), eye.shape), 0.0), axis=1, keepdims=True)
    new_state = dcol * state + _dot_tn(k_dec, v)

    o = o_inter + o_intra
    ms = jnp.mean(o * o, axis=1, keepdims=True)
    o = o * lax.rsqrt(ms + NORM_EPS) * ng
    o = o * _sigmoid(g.astype(f32))
    return o.astype(bf16), new_state


def _hgrn_kernel(q_ref, f_ref, i_ref, g_ref, lbl_ref, ng_ref, tril_ref, lvl_ref, o_ref, st_ref, *, chunk, n_chunks):
    @pl.when(pl.program_id(1) == 0)
    def _():
        st_ref[...] = jnp.zeros_like(st_ref)

    lbl = lbl_ref[...]
    ex = jnp.exp(lbl - jnp.max(lbl, axis=0, keepdims=True))
    lb_all = ex[0:1, :] / jnp.sum(ex, axis=0, keepdims=True)
    ng_all = ng_ref[...]
    tril = tril_ref[...]
    lvl = lvl_ref[...]
    row = lax.broadcasted_iota(i32, (chunk, HG_DK), 0)
    eye = lax.broadcasted_iota(i32, (HG_DK, HG_DV), 0) == lax.broadcasted_iota(i32, (HG_DK, HG_DV), 1)

    def body(ci, carry):
        r0 = pl.multiple_of(ci * chunk, chunk)
        for h in range(HG_HEADS):
            cs = pl.ds(h * HG_DK, HG_DK)
            out, new_state = _hgrn_chunk_head(
                q_ref[pl.ds(r0, chunk), cs], f_ref[pl.ds(r0, chunk), cs],
                i_ref[pl.ds(r0, chunk), cs], g_ref[pl.ds(r0, chunk), cs],
                lb_all[:, h * HG_DK:(h + 1) * HG_DK], ng_all[:, h * HG_DV:(h + 1) * HG_DV],
                st_ref[h], tril, lvl, row, eye)
            o_ref[pl.ds(r0, chunk), cs] = out
            st_ref[h] = new_state
        return carry

    lax.fori_loop(0, n_chunks, body, 0)


def _hgrn_consts(chunk):
    t = np.arange(chunk)
    tril = (t[None, :] <= t[:, None]).astype(np.float32)
    x = t[:, None] ^ t[None, :]
    lvl = np.where(x > 0, np.floor(np.log2(np.maximum(x, 1))).astype(np.int32), -1)
    lvl = np.where(t[:, None] >= t[None, :], lvl, -2).astype(np.int32)
    return jnp.asarray(tril, dtype=bf16), jnp.asarray(lvl, dtype=i32)


def _hgrn(pf, pb, lb_logits, norm_g, batch, seq):
    n = batch * seq
    rows = min(HG_ROWS, seq)
    chunk = min(HG_CHUNK, rows)
    spb = seq // rows
    tril, lvl = _hgrn_consts(chunk)
    lvl = lvl[:min(chunk, LANES), :min(chunk, LANES)]
    row_blk = lambda b, s: b * spb + s
    return pl.pallas_call(
        functools.partial(_hgrn_kernel, chunk=chunk, n_chunks=rows // chunk),
        grid=(batch, spb),
        in_specs=[
            pl.BlockSpec((rows, HG_KEY), lambda b, s: (row_blk(b, s), 0)),
            pl.BlockSpec((rows, HG_KEY), lambda b, s: (row_blk(b, s), 1)),
            pl.BlockSpec((rows, HG_WIDTH), lambda b, s: (row_blk(b, s), PB_HI // HG_WIDTH)),
            pl.BlockSpec((rows, HG_WIDTH), lambda b, s: (row_blk(b, s), PB_HG // HG_WIDTH)),
            pl.BlockSpec(lb_logits.shape, lambda b, s: (0, 0)),
            pl.BlockSpec((1, HG_WIDTH), lambda b, s: (0, 0)),
            pl.BlockSpec((chunk, chunk), lambda b, s: (0, 0)),
            pl.BlockSpec(lvl.shape, lambda b, s: (0, 0)),
        ],
        out_specs=pl.BlockSpec((rows, HG_WIDTH), lambda b, s: (row_blk(b, s), 0)),
        out_shape=jax.ShapeDtypeStruct((n, HG_WIDTH), bf16),
        scratch_shapes=[pltpu.VMEM((HG_HEADS, HG_DK, HG_DV), f32)],
        compiler_params=pltpu.CompilerParams(
            dimension_semantics=("parallel", "arbitrary"), vmem_limit_bytes=VMEM_LIMIT_BYTES),
        name="hgrn2",
    )(pf, pf, pb, pb, lb_logits, norm_g, tril, lvl)


def _attn_kernel(lam_ref, ng_ref, q_ref, k_ref, v_ref, o_ref, qs_ref, m_ref, acc_ref, *, t):
    qi = pl.program_id(1)
    blk = 2 * DA_HD

    lane = lax.broadcasted_iota(i32, (t, blk), 1)
    for h in range(DA_HEADS):
        q = q_ref[:, h * blk:(h + 1) * blk] * jnp.asarray(DA_HD ** -0.5, bf16)
        zero = jnp.zeros_like(q)
        qs_ref[h, 0:t, :] = jnp.where(lane < DA_HD, q, zero)
        qs_ref[h, t:2 * t, :] = jnp.where(lane >= DA_HD, q, zero)
    m_ref[...] = jnp.full_like(m_ref, -jnp.inf)
    acc_ref[...] = jnp.zeros_like(acc_ref)

    col = lax.broadcasted_iota(i32, (1, t), 1)
    ones = jnp.ones((t, DA_DV), bf16)

    def step(kt, masked):
        k0 = pl.multiple_of(kt * t, t)
        rel = ((kt - qi) * t + col).astype(f32)
        if masked:
            rr = lax.broadcasted_iota(i32, (2 * t, t), 0)
            rr = jnp.where(rr >= t, rr - t, rr)
            causal = lax.broadcasted_iota(i32, (2 * t, t), 1) <= rr
        for h in range(DA_HEADS):
            cs = pl.ds(h * blk, blk)
            k = k_ref[pl.ds(k0, t), cs]
            vo = jnp.concatenate([v_ref[pl.ds(k0, t), cs], ones], axis=1)
            bias = (2.0 ** (-8.0 * (h + 1) / DA_HEADS)) * rel
            for r0 in range(0, 2 * t, ATT_R):
                rows = slice(r0, r0 + ATT_R)
                s = _dot_nt(qs_ref[h, rows, :], k) + bias
                if masked:
                    s = jnp.where(causal[rows, :], s, -jnp.inf)
                m_old = m_ref[h, rows, :]
                m_new = jnp.maximum(m_old, jnp.max(s, axis=1, keepdims=True))
                alpha = jnp.exp(m_old - m_new)
                p = jnp.exp(s - jnp.concatenate([m_new] * (t // DA_DV), axis=1)).astype(bf16)
                acc_ref[h, rows, :] = jnp.concatenate([alpha, alpha], axis=1) * acc_ref[h, rows, :] + _dot(p, vo)
                m_ref[h, rows, :] = m_new

    def loop_body(kp, carry):
        for u in range(ATT_U):
            step(ATT_U * kp + u, False)
        return carry

    lax.fori_loop(0, qi // ATT_U, loop_body, 0)

    def tail_body(kt, carry):
        step(kt, False)
        return carry

    lax.fori_loop((qi // ATT_U) * ATT_U, qi, tail_body, 0)

    step(qi, True)

    lp = lam_ref[...]
    lam = (jnp.exp(jnp.sum(lp[0:1, :] * lp[1:2, :], axis=1, keepdims=True))
           - jnp.exp(jnp.sum(lp[2:3, :] * lp[3:4, :], axis=1, keepdims=True)) + LAMBDA_INIT)
    for h in range(DA_HEADS):
        acc = acc_ref[h]
        o_all = acc[:, 0:DA_DV] / acc[:, DA_DV:2 * DA_DV]
        o = o_all[0:t, :] - lam * o_all[t:2 * t, :]
        ms = jnp.mean(o * o, axis=1, keepdims=True)
        o = o * lax.rsqrt(ms + NORM_EPS) * ng_ref[:, h * DA_DV:(h + 1) * DA_DV] * (1.0 - LAMBDA_INIT)
        o_ref[:, h * DA_DV:(h + 1) * DA_DV] = o.astype(bf16)


def _attn(pb, lam_params, norm_g, batch, seq):
    n = batch * seq
    t = min(ATT_T, seq)
    nq = seq // t
    q0, k0, v0 = PB_DQ // DA_QK, PB_DK // DA_QK, PB_DV // DA_WIDTH
    return pl.pallas_call(
        functools.partial(_attn_kernel, t=t),
        grid=(batch, nq),
        in_specs=[
            pl.BlockSpec(lam_params.shape, lambda b, i: (0, 0)),
            pl.BlockSpec((1, DA_WIDTH), lambda b, i: (0, 0)),
            pl.BlockSpec((t, DA_QK), lambda b, i: (b * nq + i, q0)),
            pl.BlockSpec((seq, DA_QK), lambda b, i: (b, k0)),
            pl.BlockSpec((seq, DA_WIDTH), lambda b, i: (b, v0)),
        ],
        out_specs=pl.BlockSpec((t, DA_WIDTH), lambda b, i: (b * nq + i, 0)),
        out_shape=jax.ShapeDtypeStruct((n, DA_WIDTH), bf16),
        scratch_shapes=[pltpu.VMEM((DA_HEADS, 2 * t, 2 * DA_HD), bf16), pltpu.VMEM((DA_HEADS, 2 * t, DA_DV), f32),
                        pltpu.VMEM((DA_HEADS, 2 * t, 2 * DA_DV), f32)],
        compiler_params=pltpu.CompilerParams(
            dimension_semantics=("parallel", "arbitrary"), vmem_limit_bytes=VMEM_LIMIT_BYTES),
        name="diff_attn",
    )(lam_params, norm_g.reshape(1, DA_WIDTH), pb, pb, pb)


def _layer_norm(y, g, b):
    mu = jnp.mean(y, axis=1, keepdims=True)
    yc = y - mu
    var = jnp.mean(yc * yc, axis=1, keepdims=True)
    return yc * lax.rsqrt(var + LN_EPS) * g + b


def _assemble4(cols, dtype):
    tm = cols[0].shape[0]
    lane = lax.broadcasted_iota(i32, (tm, TOP_K), 1)
    out = jnp.broadcast_to(cols[TOP_K - 1], (tm, TOP_K))
    for k in range(TOP_K - 2, -1, -1):
        out = jnp.where(lane == k, jnp.broadcast_to(cols[k], (tm, TOP_K)), out)
    return out.astype(dtype)


def _merge_kernel(oa_ref, ob_ref, ga_ref, gb_ref, x_ref, wa_ref, wb_ref, wo_ref, g1_ref, b1_ref, rw_ref, rb_ref,
                  tri_ref, h1_ref, tw_ref, code_ref, cnt_ref, carry_ref):
    @pl.when(pl.program_id(0) == 0)
    def _():
        carry_ref[...] = jnp.zeros_like(carry_ref)

    a = _dot(oa_ref[...], wa_ref[...])
    b = _dot(ob_ref[...], wb_ref[...])
    merged = _sigmoid(ga_ref[...].astype(f32)) * a + _sigmoid(gb_ref[...].astype(f32)) * b
    mix = _dot(merged.astype(bf16), wo_ref[...])
    h1 = _layer_norm(DN_ALPHA * x_ref[...] + mix, g1_ref[...], b1_ref[...])
    h1_ref[...] = h1

    logits = _dot(h1.astype(bf16), rw_ref[...]) + rb_ref[...]
    tm = logits.shape[0]
    lane = lax.broadcasted_iota(i32, (tm, N_EXPERTS), 1).astype(f32)
    work = logits
    vals, idxs = [], []
    for _ in range(TOP_K):
        mk = jnp.max(work, axis=1, keepdims=True)
        ik = jnp.min(jnp.where(work == mk, lane, float(N_EXPERTS)), axis=1, keepdims=True)
        vals.append(mk)
        idxs.append(ik)
        work = jnp.where(lane == ik, -jnp.inf, work)
    es = [jnp.exp(v - vals[0]) for v in vals]
    den = es[0] + es[1] + es[2] + es[3]
    tw_ref[...] = _assemble4([e / den for e in es], f32)

    onehot = jnp.zeros((tm, N_EXPERTS), f32)
    for ik in idxs:
        onehot = onehot + jnp.where(lane == ik, 1.0, 0.0)
    before = _dot(tri_ref[...], onehot.astype(bf16)) + carry_ref[...]
    ranks = [jnp.sum(jnp.where(lane == ik, before, 0.0), axis=1, keepdims=True) for ik in idxs]
    code_ref[...] = _assemble4([ik * float(2 ** RANK_BITS) + rk for ik, rk in zip(idxs, ranks)], i32)
    total = carry_ref[...] + jnp.sum(onehot, axis=0, keepdims=True)
    carry_ref[...] = total
    cnt_ref[...] = total


def _merge(o_a, o_b, pb, x2d, wa, wb, wo, g1, b1, rw, rb):
    n = x2d.shape[0]
    assert n <= 2 ** RANK_BITS
    tm = min(MERGE_TM, n)
    t = np.arange(tm)
    tri = jnp.asarray((t[None, :] < t[:, None]).astype(np.float32), dtype=bf16)
    row = lambda i: (i, 0)
    const = lambda i: (0, 0)
    return pl.pallas_call(
        _merge_kernel,
        grid=(n // tm,),
        in_specs=[
            pl.BlockSpec((tm, HG_WIDTH), row),
            pl.BlockSpec((tm, DA_WIDTH), row),
            pl.BlockSpec((tm, D_MODEL), lambda i: (i, PB_GA // D_MODEL)),
            pl.BlockSpec((tm, D_MODEL), lambda i: (i, PB_GB // D_MODEL)),
            pl.BlockSpec((tm, D_MODEL), row),
            pl.BlockSpec(wa.shape, const), pl.BlockSpec(wb.shape, const), pl.BlockSpec(wo.shape, const),
            pl.BlockSpec(g1.shape, const), pl.BlockSpec(b1.shape, const),
            pl.BlockSpec(rw.shape, const), pl.BlockSpec(rb.shape, const),
            pl.BlockSpec((tm, tm), const),
        ],
        out_specs=[
            pl.BlockSpec((tm, D_MODEL), row),
            pl.BlockSpec((tm, TOP_K), row), pl.BlockSpec((tm, TOP_K), row),
            pl.BlockSpec((1, N_EXPERTS), const),
        ],
        out_shape=[
            jax.ShapeDtypeStruct((n, D_MODEL), f32),
            jax.ShapeDtypeStruct((n, TOP_K), f32), jax.ShapeDtypeStruct((n, TOP_K), i32),
            jax.ShapeDtypeStruct((1, N_EXPERTS), f32),
        ],
        scratch_shapes=[pltpu.VMEM((1, N_EXPERTS), f32)],
        compiler_params=pltpu.CompilerParams(
            dimension_semantics=("arbitrary",), vmem_limit_bytes=VMEM_LIMIT_BYTES),
        name="merge_ln_router",
    )(o_a, o_b, pb, pb, x2d, wa, wb, wo, g1, b1, rw, rb, tri)


def _row_copy_loop(n_rows, start_one):
    def body(t, carry):
        for k in range(TOP_K):
            start_one(t, k)
        return carry
    lax.fori_loop(0, n_rows, body, 0)


def _to_row_tiles(dst_ref, x):
    rows = x.shape[0]
    for c in range(ROW_SUB):
        dst_ref[pl.ds(c, rows, stride=ROW_SUB), :] = x[:, c * LANES:(c + 1) * LANES]


def _from_row_tiles(src_ref, rows):
    return jnp.concatenate([src_ref[pl.ds(c, rows, stride=ROW_SUB), :] for c in range(ROW_SUB)], axis=1)


def _tile_rows(r, n=1):
    return pl.ds(pl.multiple_of(r * ROW_SUB, ROW_SUB), n * ROW_SUB)


def _sorted_row(offs_ref, code):
    return offs_ref[lax.shift_right_logical(code, RANK_BITS)] + (code & (2 ** RANK_BITS - 1))


def _dispatch_kernel(zrow_ref, zflag_ref, offs_ref, code_ref, h1_ref, xs_ref, src_ref, zeros_ref, sems, zsem, *,
                     n_steps):
    tm = h1_ref.shape[0]
    i = pl.program_id(0)
    slot = i % 2

    @pl.when(i == 0)
    def _():
        zeros_ref[...] = jnp.zeros_like(zeros_ref)
        for e in range(2 * N_EXPERTS):
            @pl.when(zflag_ref[e] == 1)
            def _():
                cp = pltpu.make_async_copy(zeros_ref, xs_ref.at[_tile_rows(zrow_ref[e], MOE_T)], zsem)
                cp.start()
                cp.wait()

    def row_copy(s, t, k):
        return pltpu.make_async_copy(src_ref.at[s, _tile_rows(t)],
                                     xs_ref.at[_tile_rows(_sorted_row(offs_ref, code_ref[t, k]))], sems.at[s])

    def wait_slot(s):
        _row_copy_loop(tm, lambda t, k: row_copy(s, 0, 0).wait())

    @pl.when(i >= 2)
    def _():
        wait_slot(slot)

    _to_row_tiles(src_ref.at[slot], h1_ref[...])
    _row_copy_loop(tm, lambda t, k: row_copy(slot, t, k).start())

    @pl.when(i == n_steps - 1)
    def _():
        if n_steps >= 2:
            wait_slot(1 - slot)
        wait_slot(slot)


def _dispatch(h1, code, offs, zrow, zflag, n_rows_sorted):
    n = h1.shape[0]
    tm = min(ROW_TM, n)
    return pl.pallas_call(
        functools.partial(_dispatch_kernel, n_steps=n // tm),
        grid_spec=pltpu.PrefetchScalarGridSpec(
            num_scalar_prefetch=3,
            grid=(n // tm,),
            in_specs=[
                pl.BlockSpec((tm, TOP_K), lambda i, zr, zf, of: (i, 0), memory_space=pltpu.SMEM),
                pl.BlockSpec((tm, D_MODEL), lambda i, zr, zf, of: (i, 0)),
            ],
            out_specs=pl.BlockSpec(memory_space=pl.ANY),
            scratch_shapes=[pltpu.VMEM((2, tm * ROW_SUB, LANES), f32), pltpu.VMEM((MOE_T * ROW_SUB, LANES), f32),
                            pltpu.SemaphoreType.DMA((2,)), pltpu.SemaphoreType.DMA],
        ),
        out_shape=jax.ShapeDtypeStruct((n_rows_sorted * ROW_SUB, LANES), f32),
        compiler_params=pltpu.CompilerParams(
            dimension_semantics=("arbitrary",), vmem_limit_bytes=VMEM_LIMIT_BYTES),
        name="moe_dispatch",
    )(zrow, zflag, offs, code, h1)


def _ffn_kernel(te_ref, nt_ref, seg_ref, nxt_ref, x_ref, wgu_hbm, wd_hbm, bgu_ref, bd_ref, perm_ref, y_ref,
                wgu_f, wd_f, wgu_s, wd_s, sems):
    j = pl.program_id(0)

    def weight_copies(e, s):
        return (pltpu.make_async_copy(wgu_hbm.at[0, e], wgu_f.at[s], sems.at[s, 0]),
                pltpu.make_async_copy(wd_hbm.at[0, e], wd_f.at[s], sems.at[s, 1]))

    @pl.when(j < nt_ref[0])
    def _():
        slot = seg_ref[j] % 2

        @pl.when(j == 0)
        def _():
            for cp in weight_copies(te_ref[0], 0):
                cp.start()

        @pl.when(jnp.logical_or(j == 0, te_ref[j] != te_ref[jnp.maximum(j - 1, 0)]))
        def _():
            for cp in weight_copies(te_ref[j], slot):
                cp.wait()

            @pl.when(nxt_ref[j] >= 0)
            def _():
                for cp in weight_copies(nxt_ref[j], 1 - slot):
                    cp.start()

            for r0 in range(0, D_MODEL, LANES):
                wgu_s[r0:r0 + LANES, :] = wgu_f[slot, r0:r0 + LANES, :].astype(bf16)
            for r0 in range(0, D_FF, LANES):
                wd_s[r0:r0 + LANES, :] = _dot(perm_ref[...], wd_f[slot, r0:r0 + LANES, :].astype(bf16)).astype(bf16)

        x = _from_row_tiles(x_ref, MOE_T).astype(bf16)
        acc = jnp.broadcast_to(bd_ref[0], (MOE_T, D_MODEL))
        even = (lax.broadcasted_iota(i32, (MOE_T, LANES), 1) & 1) == 0
        for c0 in range(0, D_FF, FFN_FC):
            cols = slice(2 * c0, 2 * (c0 + FFN_FC))
            hgu = _dot(x, wgu_s[:, cols]) + bgu_ref[0, :, cols]
            zs = []
            for j0 in range(0, 2 * FFN_FC, 2 * LANES):
                lo = hgu[:, j0:j0 + LANES]
                hi = hgu[:, j0 + LANES:j0 + 2 * LANES]
                gate = jnp.where(even, lo, pltpu.roll(hi, 1, axis=1))
                up = jnp.where(even, pltpu.roll(lo, LANES - 1, axis=1), hi)
                gate = jnp.minimum(gate, SWIGLU_LIMIT)
                up = jnp.clip(up, -SWIGLU_LIMIT, SWIGLU_LIMIT)
                glu = gate * _sigmoid(gate * SWIGLU_ALPHA)
                zs.append(((up + 1.0) * glu).astype(bf16))
            acc = acc + _dot(jnp.concatenate(zs, axis=1), wd_s[c0:c0 + FFN_FC, :])
        _to_row_tiles(y_ref, acc)

    @pl.when(j >= nt_ref[0])
    def _():
        y_ref[...] = jnp.zeros_like(y_ref)


def _ffn(xs, tile_e, n_tiles, seg, nxt, wgu, wd, bgu, bd):
    p = xs.shape[0] // ROW_SUB
    last = lambda j, te, nt, sg, nx: jnp.minimum(j, nt[0] - 1)
    bmap = lambda j, te, nt, sg, nx: (te[j], 0, 0)
    r = np.arange(LANES)
    perm = np.zeros((LANES, LANES), np.float32)
    perm[r, (r % 2) * (LANES // 2) + r // 2] = 1.0
    return pl.pallas_call(
        _ffn_kernel,
        grid_spec=pltpu.PrefetchScalarGridSpec(
            num_scalar_prefetch=4,
            grid=(p // MOE_T,),
            in_specs=[
                pl.BlockSpec((MOE_T * ROW_SUB, LANES), lambda j, te, nt, sg, nx: (last(j, te, nt, sg, nx), 0)),
                pl.BlockSpec(memory_space=pl.ANY),
                pl.BlockSpec(memory_space=pl.ANY),
                pl.BlockSpec((1, 1, 2 * D_FF), bmap),
                pl.BlockSpec((1, 1, D_MODEL), bmap),
                pl.BlockSpec((LANES, LANES), lambda j, te, nt, sg, nx: (0, 0)),
            ],
            out_specs=pl.BlockSpec((MOE_T * ROW_SUB, LANES), lambda j, te, nt, sg, nx: (j, 0)),
            scratch_shapes=[pltpu.VMEM((2, D_MODEL, 2 * D_FF), f32), pltpu.VMEM((2, D_FF, D_MODEL), f32),
                            pltpu.VMEM((D_MODEL, 2 * D_FF), bf16), pltpu.VMEM((D_FF, D_MODEL), bf16),
                            pltpu.SemaphoreType.DMA((2, 2))],
        ),
        out_shape=jax.ShapeDtypeStruct((p * ROW_SUB, LANES), f32),
        compiler_params=pltpu.CompilerParams(
            dimension_semantics=("arbitrary",), vmem_limit_bytes=VMEM_LIMIT_BYTES),
        name="moe_ffn",
    )(tile_e, n_tiles, seg, nxt, xs, wgu, wd, bgu, bd, jnp.asarray(perm, dtype=bf16))


def _combine_kernel(offs_ref, code_ref, nxt_ref, h1_ref, tw_ref, g2_ref, b2_ref, ys_ref, o_ref, ybuf_ref, sems, *,
                    n_steps):
    tm = h1_ref.shape[0]
    i = pl.program_id(0)
    slot = i % 2

    def row_copy(c_ref, s, t, k):
        return pltpu.make_async_copy(ys_ref.at[_tile_rows(_sorted_row(offs_ref, c_ref[t, k]))],
                                     ybuf_ref.at[s, k, _tile_rows(t)], sems.at[s])

    @pl.when(i == 0)
    def _():
        _row_copy_loop(tm, lambda t, k: row_copy(code_ref, 0, t, k).start())

    @pl.when(i + 1 < n_steps)
    def _():
        _row_copy_loop(tm, lambda t, k: row_copy(nxt_ref, 1 - slot, t, k).start())

    _row_copy_loop(tm, lambda t, k: row_copy(code_ref, slot, 0, 0).wait())

    tw = tw_ref[...]
    ffn = tw[:, 0:1] * _from_row_tiles(ybuf_ref.at[slot, 0], tm)
    for k in range(1, TOP_K):
        ffn = ffn + tw[:, k:k + 1] * _from_row_tiles(ybuf_ref.at[slot, k], tm)
    o_ref[...] = _layer_norm(DN_ALPHA * h1_ref[...] + ffn, g2_ref[...], b2_ref[...])


def _combine(h1, tw, code, offs, ys, g2, b2):
    n = h1.shape[0]
    tm = min(ROW_TM, n)
    n_steps = n // tm
    return pl.pallas_call(
        functools.partial(_combine_kernel, n_steps=n_steps),
        grid_spec=pltpu.PrefetchScalarGridSpec(
            num_scalar_prefetch=1,
            grid=(n_steps,),
            in_specs=[
                pl.BlockSpec((tm, TOP_K), lambda i, of: (i, 0), memory_space=pltpu.SMEM),
                pl.BlockSpec((tm, TOP_K), lambda i, of: (jnp.minimum(i + 1, n_steps - 1), 0),
                             memory_space=pltpu.SMEM),
                pl.BlockSpec((tm, D_MODEL), lambda i, of: (i, 0)),
                pl.BlockSpec((tm, TOP_K), lambda i, of: (i, 0)),
                pl.BlockSpec(g2.shape, lambda i, of: (0, 0)), pl.BlockSpec(b2.shape, lambda i, of: (0, 0)),
                pl.BlockSpec(memory_space=pl.ANY),
            ],
            out_specs=pl.BlockSpec((tm, D_MODEL), lambda i, of: (i, 0)),
            scratch_shapes=[pltpu.VMEM((2, TOP_K, tm * ROW_SUB, LANES), f32), pltpu.SemaphoreType.DMA((2,))],
        ),
        out_shape=jax.ShapeDtypeStruct((n, D_MODEL), f32),
        compiler_params=pltpu.CompilerParams(
            dimension_semantics=("arbitrary",), vmem_limit_bytes=VMEM_LIMIT_BYTES),
        name="moe_combine_ln",
    )(offs, code, code, h1, tw, g2, b2, ys)


def _routing_tables(counts, n_tiles_max):
    padded = ((counts + MOE_T - 1) // MOE_T) * MOE_T
    ends = jnp.cumsum(padded)
    offs = ends - padded
    n_tiles = (ends[-1] // MOE_T).astype(i32)
    tile_start = jnp.minimum(jnp.arange(n_tiles_max, dtype=i32), n_tiles - 1) * MOE_T
    tile_e = jnp.minimum(jnp.sum((ends[None, :] <= tile_start[:, None]).astype(i32), axis=1), N_EXPERTS - 1)
    trailing = jnp.arange(n_tiles_max - N_EXPERTS, n_tiles_max, dtype=i32)
    zrow = jnp.concatenate([(ends - MOE_T).astype(i32), trailing * MOE_T])
    zflag = jnp.concatenate([(padded > counts).astype(i32), (trailing >= n_tiles).astype(i32)])
    seg = jnp.cumsum(jnp.concatenate([jnp.zeros((1,), i32), (tile_e[1:] != tile_e[:-1]).astype(i32)]))
    e_ids = jnp.arange(N_EXPERTS, dtype=i32)
    later = (e_ids[None, :] > e_ids[:, None]) & (counts[None, :] > 0)
    next_e = jnp.min(jnp.where(later, e_ids[None, :], N_EXPERTS), axis=1)
    nxt = jnp.where(next_e < N_EXPERTS, next_e, -1)[tile_e]
    return offs.astype(i32), tile_e, n_tiles.reshape(1), zrow, zflag, seg.astype(i32), nxt.astype(i32)


def kernel(x, w_in, hg_lb_logits, hg_norm_g, da_lambda, da_norm_g, w_branch_a, w_branch_b, w_out, ln1_g, ln1_b,
           router_w, router_b, w_gate_up, b_gate_up, w_down, b_down, ln2_g, ln2_b):
    batch, seq, d = x.shape
    assert d == D_MODEL and w_in.shape[0] == DEPTH == 1
    n = batch * seq
    x2d = x.reshape(n, d)

    w = w_in[0]
    o_hi, o_ga = 2 * HG_KEY, 2 * HG_KEY + 2 * HG_WIDTH + 2 * DA_QK + DA_WIDTH
    w_all = jnp.concatenate([w[:, 0:o_hi], w[:, o_ga:], w[:, o_hi:o_ga]], axis=1).astype(bf16)
    pf, pb = _inproj(x2d, w_all, o_hi)

    o_a = _hgrn(pf, pb, hg_lb_logits, hg_norm_g[0].reshape(1, HG_WIDTH), batch, seq)
    o_b = _attn(pb, da_lambda[0], da_norm_g[0], batch, seq)

    h1, tw, code, cnt = _merge(
        o_a, o_b, pb, x2d, w_branch_a[0].astype(bf16), w_branch_b[0].astype(bf16), w_out[0].astype(bf16),
        ln1_g[0].reshape(1, d), ln1_b[0].reshape(1, d), router_w[0].astype(bf16), router_b[0].reshape(1, N_EXPERTS))

    n_rows_sorted = n * TOP_K + N_EXPERTS * MOE_T
    offs, tile_e, n_tiles, zrow, zflag, seg, nxt = _routing_tables(cnt[0].astype(i32), n_rows_sorted // MOE_T)

    xs = _dispatch(h1, code, offs, zrow, zflag, n_rows_sorted)
    ys = _ffn(xs, tile_e, n_tiles, seg, nxt, w_gate_up, w_down, b_gate_up[0][:, None, :], b_down[0][:, None, :])
    out = _combine(h1, tw, code, offs, ys, ln2_g[0].reshape(1, d), ln2_b[0].reshape(1, d))
    return out.reshape(batch, seq, d)
```

```python
import functools
import math

import jax
import jax.numpy as jnp
import numpy as np
from jax import lax
from jax.experimental import pallas as pl
from jax.experimental.pallas import tpu as pltpu

f32 = jnp.float32
bf16 = jnp.bfloat16
i32 = jnp.int32

D_MODEL = 1024
DEPTH = 1
HG_HEADS = 4
HG_DK = 128
HG_DV = 128
HG_KEY = HG_HEADS * HG_DK
HG_WIDTH = HG_HEADS * HG_DV
DA_HEADS = 4
DA_HD = 64
DA_DV = 2 * DA_HD
DA_QK = DA_HEADS * 2 * DA_HD
DA_WIDTH = DA_HEADS * DA_DV
N_EXPERTS = 32
TOP_K = 4
D_FF = 1024
SWIGLU_LIMIT = 7.0
SWIGLU_ALPHA = 1.702
DN_ALPHA = (2.0 * DEPTH) ** 0.25
LN_EPS = 1e-5
NORM_EPS = 1e-6
LAMBDA_INIT = 0.8 - 0.6 * math.exp(-0.3 * 0)
LOG2_E = 1.0 / math.log(2.0)

VMEM_LIMIT_BYTES = 52 * 1024 * 1024

PROJ_TM = 2048
PROJ_TN = 512
HG_CHUNK = 256
HG_ROWS = 512
ATT_T = 256
ATT_R = 128
ATT_U = 4
MERGE_TM = 1024
MOE_T = 256
ROW_TM = 256
FFN_FC = 1024
LANES = 128
RANK_BITS = 17
ROW_SUB = D_MODEL // LANES

PB_GA = 0
PB_GB = PB_GA + D_MODEL
PB_HI = PB_GB + D_MODEL
PB_HG = PB_HI + HG_WIDTH
PB_DQ = PB_HG + HG_WIDTH
PB_DK = PB_DQ + DA_QK
PB_DV = PB_DK + DA_QK


def _sigmoid(x):
    return 0.5 * jnp.tanh(0.5 * x) + 0.5


def _dot(a, b):
    return jnp.dot(a, b, preferred_element_type=f32)


def _dot_nt(a, b):
    return lax.dot_general(a, b, (((1,), (1,)), ((), ())), preferred_element_type=f32)


def _dot_tn(a, b):
    return lax.dot_general(a, b, (((0,), (0,)), ((), ())), preferred_element_type=f32)


def _inproj_kernel(x_ref, w_ref, of_ref, ob_ref, xb_ref, *, n_f32_tiles):
    j = pl.program_id(1)

    @pl.when(j == 0)
    def _():
        xb_ref[...] = x_ref[...].astype(bf16)

    r = _dot(xb_ref[...], w_ref[...])

    @pl.when(j < n_f32_tiles)
    def _():
        of_ref[...] = r

    @pl.when(j >= n_f32_tiles)
    def _():
        ob_ref[...] = r.astype(bf16)


def _inproj(x2d, w, n_f32_cols):
    n, k = x2d.shape
    m = w.shape[1]
    tm = min(PROJ_TM, n)
    nf = n_f32_cols // PROJ_TN
    return pl.pallas_call(
        functools.partial(_inproj_kernel, n_f32_tiles=nf),
        grid=(n // tm, m // PROJ_TN),
        in_specs=[pl.BlockSpec((tm, k), lambda i, j: (i, 0)),
                  pl.BlockSpec((k, PROJ_TN), lambda i, j: (0, j))],
        out_specs=[pl.BlockSpec((tm, PROJ_TN), lambda i, j: (i, jnp.minimum(j, nf - 1))),
                   pl.BlockSpec((tm, PROJ_TN), lambda i, j: (i, jnp.maximum(j - nf, 0)))],
        out_shape=[jax.ShapeDtypeStruct((n, n_f32_cols), f32), jax.ShapeDtypeStruct((n, m - n_f32_cols), bf16)],
        scratch_shapes=[pltpu.VMEM((tm, k), bf16)],
        compiler_params=pltpu.CompilerParams(
            dimension_semantics=("parallel", "arbitrary"), vmem_limit_bytes=VMEM_LIMIT_BYTES),
        name="inproj",
    )(x2d, w)


def _shift_down(x, d):
    n = x.shape[0]
    if d % 8 == 0:
        return jnp.concatenate([x[n - d:], x[:n - d]], axis=0)
    return pltpu.roll(x, d, axis=0)


def _shift_up(x, d):
    n = x.shape[0]
    if d % 8 == 0:
        return jnp.concatenate([x[d:], x[:d]], axis=0)
    return pltpu.roll(x, n - d, axis=0)


def _hgrn_chunk_head(qv, fl, v, g, lb, ng, state, tril, lvl, row, eye):
    c = qv.shape[0]
    f = lb + (1.0 - lb) * _sigmoid(fl)
    logf = jnp.log(f)
    kk = 1.0 - f
    qa = qv * _sigmoid(qv)

    l1 = logf.astype(bf16)
    r1 = logf - l1.astype(f32)
    l2 = r1.astype(bf16)
    l3 = (r1 - l2.astype(f32)).astype(bf16)
    bc = _dot(tril, jnp.concatenate([l1, l2, l3], axis=1))
    b = bc[:, 0:HG_DK] + bc[:, HG_DK:2 * HG_DK] + bc[:, 2 * HG_DK:3 * HG_DK]

    blk = lvl.shape[0]
    nblk = c // blk
    assert nblk in (1, 2)
    qa_b, kk_b = qa.astype(bf16), kk.astype(bf16)
    diag = [jnp.where(lvl == -1, _dot_nt(qa_b[i * blk:(i + 1) * blk], kk_b[i * blk:(i + 1) * blk]), 0.0)
            for i in range(nblk)]
    off_diag = None
    filled = b
    d = 1
    level = 0
    while d < c:
        upper = (row & d) != 0
        ref_b = jnp.where(upper, _shift_down(filled, d), filled)
        e = jnp.exp2(jnp.abs(b - ref_b) * (-LOG2_E))
        mixed = (jnp.where(upper, qa, kk) * e).astype(bf16)
        if d < blk:
            for i in range(nblk):
                m_i = mixed[i * blk:(i + 1) * blk]
                diag[i] = jnp.where(lvl == level, _dot_nt(m_i, m_i), diag[i])
        else:
            off_diag = _dot_nt(mixed[d:2 * d], mixed[0:d])
        filled = jnp.where(upper, filled, _shift_up(filled, d))
        d *= 2
        level += 1
    if nblk == 1:
        attn = diag[0]
    else:
        attn = jnp.concatenate([jnp.concatenate([diag[0], jnp.zeros_like(diag[0])], axis=1),
                                jnp.concatenate([off_diag, diag[1]], axis=1)], axis=0)

    o_intra = _dot(attn.astype(bf16), v)
    o_inter = _dot((qa * jnp.exp(b)).astype(bf16), state.astype(bf16))

    b_last = b[c - 1:c, :]
    k_dec = (kk * jnp.exp(b_last - b)).astype(bf16)
    dcol = jnp.sum(jnp.where(eye, jnp.broadcast_to(jnp.exp(b_last), eye.shape), 0.0), axis=1, keepdims=True)
    new_state = dcol * state + _dot_tn(k_dec, v)

    o = o_inter + o_intra
    ms = jnp.mean(o * o, axis=1, keepdims=True)
    o = o * lax.rsqrt(ms + NORM_EPS) * ng
    o = o * _sigmoid(g.astype(f32))
    return o.astype(bf16), new_state


def _hgrn_kernel(q_ref, f_ref, i_ref, g_ref, lbl_ref, ng_ref, tril_ref, lvl_ref, o_ref, st_ref, *, chunk, n_chunks):
    @pl.when(pl.program_id(1) == 0)
    def _():
        st_ref[...] = jnp.zeros_like(st_ref)

    lbl = lbl_ref[...]
    ex = jnp.exp(lbl - jnp.max(lbl, axis=0, keepdims=True))
    lb_all = ex[0:1, :] / jnp.sum(ex, axis=0, keepdims=True)
    ng_all = ng_ref[...]
    tril = tril_ref[...]
    lvl = lvl_ref[...]
    row = lax.broadcasted_iota(i32, (chunk, HG_DK), 0)
    eye = lax.broadcasted_iota(i32, (HG_DK, HG_DV), 0) == lax.broadcasted_iota(i32, (HG_DK, HG_DV), 1)

    def body(ci, carry):
        r0 = pl.multiple_of(ci * chunk, chunk)
        for h in range(HG_HEADS):
            cs = pl.ds(h * HG_DK, HG_DK)
            out, new_state = _hgrn_chunk_head(
                q_ref[pl.ds(r0, chunk), cs], f_ref[pl.ds(r0, chunk), cs],
                i_ref[pl.ds(r0, chunk), cs], g_ref[pl.ds(r0, chunk), cs],
                lb_all[:, h * HG_DK:(h + 1) * HG_DK], ng_all[:, h * HG_DV:(h + 1) * HG_DV],
                st_ref[h], tril, lvl, row, eye)
            o_ref[pl.ds(r0, chunk), cs] = out
            st_ref[h] = new_state
        return carry

    lax.fori_loop(0, n_chunks, body, 0)


def _hgrn_consts(chunk):
    t = np.arange(chunk)
    tril = (t[None, :] <= t[:, None]).astype(np.float32)
    x = t[:, None] ^ t[None, :]
    lvl = np.where(x > 0, np.floor(np.log2(np.maximum(x, 1))).astype(np.int32), -1)
    lvl = np.where(t[:, None] >= t[None, :], lvl, -2).astype(np.int32)
    return jnp.asarray(tril, dtype=bf16), jnp.asarray(lvl, dtype=i32)


def _hgrn(pf, pb, lb_logits, norm_g, batch, seq):
    n = batch * seq
    rows = min(HG_ROWS, seq)
    chunk = min(HG_CHUNK, rows)
    spb = seq // rows
    tril, lvl = _hgrn_consts(chunk)
    lvl = lvl[:min(chunk, LANES), :min(chunk, LANES)]
    row_blk = lambda b, s: b * spb + s
    return pl.pallas_call(
        functools.partial(_hgrn_kernel, chunk=chunk, n_chunks=rows // chunk),
        grid=(batch, spb),
        in_specs=[
            pl.BlockSpec((rows, HG_KEY), lambda b, s: (row_blk(b, s), 0)),
            pl.BlockSpec((rows, HG_KEY), lambda b, s: (row_blk(b, s), 1)),
            pl.BlockSpec((rows, HG_WIDTH), lambda b, s: (row_blk(b, s), PB_HI // HG_WIDTH)),
            pl.BlockSpec((rows, HG_WIDTH), lambda b, s: (row_blk(b, s), PB_HG // HG_WIDTH)),
            pl.BlockSpec(lb_logits.shape, lambda b, s: (0, 0)),
            pl.BlockSpec((1, HG_WIDTH), lambda b, s: (0, 0)),
            pl.BlockSpec((chunk, chunk), lambda b, s: (0, 0)),
            pl.BlockSpec(lvl.shape, lambda b, s: (0, 0)),
        ],
        out_specs=pl.BlockSpec((rows, HG_WIDTH), lambda b, s: (row_blk(b, s), 0)),
        out_shape=jax.ShapeDtypeStruct((n, HG_WIDTH), bf16),
        scratch_shapes=[pltpu.VMEM((HG_HEADS, HG_DK, HG_DV), f32)],
        compiler_params=pltpu.CompilerParams(
            dimension_semantics=("parallel", "arbitrary"), vmem_limit_bytes=VMEM_LIMIT_BYTES),
        name="hgrn2",
    )(pf, pf, pb, pb, lb_logits, norm_g, tril, lvl)


def _attn_kernel(lam_ref, ng_ref, q_ref, k_ref, v_ref, o_ref, qs_ref, m_ref, acc_ref, *, t):
    qi = pl.program_id(1)
    blk = 2 * DA_HD

    lane = lax.broadcasted_iota(i32, (t, blk), 1)
    for h in range(DA_HEADS):
        q = q_ref[:, h * blk:(h + 1) * blk] * jnp.asarray(DA_HD ** -0.5, bf16)
        zero = jnp.zeros_like(q)
        qs_ref[h, 0:t, :] = jnp.where(lane < DA_HD, q, zero)
        qs_ref[h, t:2 * t, :] = jnp.where(lane >= DA_HD, q, zero)
    m_ref[...] = jnp.full_like(m_ref, -jnp.inf)
    acc_ref[...] = jnp.zeros_like(acc_ref)

    col = lax.broadcasted_iota(i32, (1, t), 1)
    ones = jnp.ones((t, DA_DV), bf16)

    def step(kt, masked):
        k0 = pl.multiple_of(kt * t, t)
        rel = ((kt - qi) * t + col).astype(f32)
        if masked:
            rr = lax.broadcasted_iota(i32, (2 * t, t), 0)
            rr = jnp.where(rr >= t, rr - t, rr)
            causal = lax.broadcasted_iota(i32, (2 * t, t), 1) <= rr
        for h in range(DA_HEADS):
            cs = pl.ds(h * blk, blk)
            k = k_ref[pl.ds(k0, t), cs]
            vo = jnp.concatenate([v_ref[pl.ds(k0, t), cs], ones], axis=1)
            bias = (2.0 ** (-8.0 * (h + 1) / DA_HEADS)) * rel
            for r0 in range(0, 2 * t, ATT_R):
                rows = slice(r0, r0 + ATT_R)
                s = _dot_nt(qs_ref[h, rows, :], k) + bias
                if masked:
                    s = jnp.where(causal[rows, :], s, -jnp.inf)
                m_old = m_ref[h, rows, :]
                m_new = jnp.maximum(m_old, jnp.max(s, axis=1, keepdims=True))
                alpha = jnp.exp(m_old - m_new)
                p = jnp.exp(s - jnp.concatenate([m_new] * (t // DA_DV), axis=1)).astype(bf16)
                acc_ref[h, rows, :] = jnp.concatenate([alpha, alpha], axis=1) * acc_ref[h, rows, :] + _dot(p, vo)
                m_ref[h, rows, :] = m_new

    def loop_body(kp, carry):
        for u in range(ATT_U):
            step(ATT_U * kp + u, False)
        return carry

    lax.fori_loop(0, qi // ATT_U, loop_body, 0)

    def tail_body(kt, carry):
        step(kt, False)
        return carry

    lax.fori_loop((qi // ATT_U) * ATT_U, qi, tail_body, 0)

    step(qi, True)

    lp = lam_ref[...]
    lam = (jnp.exp(jnp.sum(lp[0:1, :] * lp[1:2, :], axis=1, keepdims=True))
           - jnp.exp(jnp.sum(lp[2:3, :] * lp[3:4, :], axis=1, keepdims=True)) + LAMBDA_INIT)
    for h in range(DA_HEADS):
        acc = acc_ref[h]
        o_all = acc[:, 0:DA_DV] / acc[:, DA_DV:2 * DA_DV]
        o = o_all[0:t, :] - lam * o_all[t:2 * t, :]
        ms = jnp.mean(o * o, axis=1, keepdims=True)
        o = o * lax.rsqrt(ms + NORM_EPS) * ng_ref[:, h * DA_DV:(h + 1) * DA_DV] * (1.0 - LAMBDA_INIT)
        o_ref[:, h * DA_DV:(h + 1) * DA_DV] = o.astype(bf16)


def _attn(pb, lam_params, norm_g, batch, seq):
    n = batch * seq
    t = min(ATT_T, seq)
    nq = seq // t
    q0, k0, v0 = PB_DQ // DA_QK, PB_DK // DA_QK, PB_DV // DA_WIDTH
    return pl.pallas_call(
        functools.partial(_attn_kernel, t=t),
        grid=(batch, nq),
        in_specs=[
            pl.BlockSpec(lam_params.shape, lambda b, i: (0, 0)),
            pl.BlockSpec((1, DA_WIDTH), lambda b, i: (0, 0)),
            pl.BlockSpec((t, DA_QK), lambda b, i: (b * nq + i, q0)),
            pl.BlockSpec((seq, DA_QK), lambda b, i: (b, k0)),
            pl.BlockSpec((seq, DA_WIDTH), lambda b, i: (b, v0)),
        ],
        out_specs=pl.BlockSpec((t, DA_WIDTH), lambda b, i: (b * nq + i, 0)),
        out_shape=jax.ShapeDtypeStruct((n, DA_WIDTH), bf16),
        scratch_shapes=[pltpu.VMEM((DA_HEADS, 2 * t, 2 * DA_HD), bf16), pltpu.VMEM((DA_HEADS, 2 * t, DA_DV), f32),
                        pltpu.VMEM((DA_HEADS, 2 * t, 2 * DA_DV), f32)],
        compiler_params=pltpu.CompilerParams(
            dimension_semantics=("parallel", "arbitrary"), vmem_limit_bytes=VMEM_LIMIT_BYTES),
        name="diff_attn",
    )(lam_params, norm_g.reshape(1, DA_WIDTH), pb, pb, pb)


def _layer_norm(y, g, b):
    mu = jnp.mean(y, axis=1, keepdims=True)
    yc = y - mu
    var = jnp.mean(yc * yc, axis=1, keepdims=True)
    return yc * lax.rsqrt(var + LN_EPS) * g + b


def _assemble4(cols, dtype):
    tm = cols[0].shape[0]
    lane = lax.broadcasted_iota(i32, (tm, TOP_K), 1)
    out = jnp.broadcast_to(cols[TOP_K - 1], (tm, TOP_K))
    for k in range(TOP_K - 2, -1, -1):
        out = jnp.where(lane == k, jnp.broadcast_to(cols[k], (tm, TOP_K)), out)
    return out.astype(dtype)


def _merge_kernel(oa_ref, ob_ref, ga_ref, gb_ref, x_ref, wa_ref, wb_ref, wo_ref, g1_ref, b1_ref, rw_ref, rb_ref,
                  tri_ref, h1_ref, tw_ref, code_ref, cnt_ref, carry_ref):
    @pl.when(pl.program_id(0) == 0)
    def _():
        carry_ref[...] = jnp.zeros_like(carry_ref)

    a = _dot(oa_ref[...], wa_ref[...])
    b = _dot(ob_ref[...], wb_ref[...])
    merged = _sigmoid(ga_ref[...].astype(f32)) * a + _sigmoid(gb_ref[...].astype(f32)) * b
    mix = _dot(merged.astype(bf16), wo_ref[...])
    h1 = _layer_norm(DN_ALPHA * x_ref[...] + mix, g1_ref[...], b1_ref[...])
    h1_ref[...] = h1

    logits = _dot(h1.astype(bf16), rw_ref[...]) + rb_ref[...]
    tm = logits.shape[0]
    lane = lax.broadcasted_iota(i32, (tm, N_EXPERTS), 1).astype(f32)
    work = logits
    vals, idxs = [], []
    for _ in range(TOP_K):
        mk = jnp.max(work, axis=1, keepdims=True)
        ik = jnp.min(jnp.where(work == mk, lane, float(N_EXPERTS)), axis=1, keepdims=True)
        vals.append(mk)
        idxs.append(ik)
        work = jnp.where(lane == ik, -jnp.inf, work)
    es = [jnp.exp(v - vals[0]) for v in vals]
    den = es[0] + es[1] + es[2] + es[3]
    tw_ref[...] = _assemble4([e / den for e in es], f32)

    onehot = jnp.zeros((tm, N_EXPERTS), f32)
    for ik in idxs:
        onehot = onehot + jnp.where(lane == ik, 1.0, 0.0)
    before = _dot(tri_ref[...], onehot.astype(bf16)) + carry_ref[...]
    ranks = [jnp.sum(jnp.where(lane == ik, before, 0.0), axis=1, keepdims=True) for ik in idxs]
    code_ref[...] = _assemble4([ik * float(2 ** RANK_BITS) + rk for ik, rk in zip(idxs, ranks)], i32)
    total = carry_ref[...] + jnp.sum(onehot, axis=0, keepdims=True)
    carry_ref[...] = total
    cnt_ref[...] = total


def _merge(o_a, o_b, pb, x2d, wa, wb, wo, g1, b1, rw, rb):
    n = x2d.shape[0]
    assert n <= 2 ** RANK_BITS
    tm = min(MERGE_TM, n)
    t = np.arange(tm)
    tri = jnp.asarray((t[None, :] < t[:, None]).astype(np.float32), dtype=bf16)
    row = lambda i: (i, 0)
    const = lambda i: (0, 0)
    return pl.pallas_call(
        _merge_kernel,
        grid=(n // tm,),
        in_specs=[
            pl.BlockSpec((tm, HG_WIDTH), row),
            pl.BlockSpec((tm, DA_WIDTH), row),
            pl.BlockSpec((tm, D_MODEL), lambda i: (i, PB_GA // D_MODEL)),
            pl.BlockSpec((tm, D_MODEL), lambda i: (i, PB_GB // D_MODEL)),
            pl.BlockSpec((tm, D_MODEL), row),
            pl.BlockSpec(wa.shape, const), pl.BlockSpec(wb.shape, const), pl.BlockSpec(wo.shape, const),
            pl.BlockSpec(g1.shape, const), pl.BlockSpec(b1.shape, const),
            pl.BlockSpec(rw.shape, const), pl.BlockSpec(rb.shape, const),
            pl.BlockSpec((tm, tm), const),
        ],
        out_specs=[
            pl.BlockSpec((tm, D_MODEL), row),
            pl.BlockSpec((tm, TOP_K), row), pl.BlockSpec((tm, TOP_K), row),
            pl.BlockSpec((1, N_EXPERTS), const),
        ],
        out_shape=[
            jax.ShapeDtypeStruct((n, D_MODEL), f32),
            jax.ShapeDtypeStruct((n, TOP_K), f32), jax.ShapeDtypeStruct((n, TOP_K), i32),
            jax.ShapeDtypeStruct((1, N_EXPERTS), f32),
        ],
        scratch_shapes=[pltpu.VMEM((1, N_EXPERTS), f32)],
        compiler_params=pltpu.CompilerParams(
            dimension_semantics=("arbitrary",), vmem_limit_bytes=VMEM_LIMIT_BYTES),
        name="merge_ln_router",
    )(o_a, o_b, pb, pb, x2d, wa, wb, wo, g1, b1, rw, rb, tri)


def _row_copy_loop(n_rows, start_one):
    def body(t, carry):
        for k in range(TOP_K):
            start_one(t, k)
        return carry
    lax.fori_loop(0, n_rows, body, 0)


def _to_row_tiles(dst_ref, x):
    rows = x.shape[0]
    for c in range(ROW_SUB):
        dst_ref[pl.ds(c, rows, stride=ROW_SUB), :] = x[:, c * LANES:(c + 1) * LANES]


def _from_row_tiles(src_ref, rows):
    return jnp.concatenate([src_ref[pl.ds(c, rows, stride=ROW_SUB), :] for c in range(ROW_SUB)], axis=1)


def _tile_rows(r, n=1):
    return pl.ds(pl.multiple_of(r * ROW_SUB, ROW_SUB), n * ROW_SUB)


def _sorted_row(offs_ref, code):
    return offs_ref[lax.shift_right_logical(code, RANK_BITS)] + (code & (2 ** RANK_BITS - 1))


def _dispatch_kernel(zrow_ref, zflag_ref, offs_ref, code_ref, h1_ref, xs_ref, src_ref, zeros_ref, sems, zsem, *,
                     n_steps):
    tm = h1_ref.shape[0]
    i = pl.program_id(0)
    slot = i % 2

    @pl.when(i == 0)
    def _():
        zeros_ref[...] = jnp.zeros_like(zeros_ref)
        for e in range(2 * N_EXPERTS):
            @pl.when(zflag_ref[e] == 1)
            def _():
                cp = pltpu.make_async_copy(zeros_ref, xs_ref.at[_tile_rows(zrow_ref[e], MOE_T)], zsem)
                cp.start()
                cp.wait()

    def row_copy(s, t, k):
        return pltpu.make_async_copy(src_ref.at[s, _tile_rows(t)],
                                     xs_ref.at[_tile_rows(_sorted_row(offs_ref, code_ref[t * TOP_K + k]))], sems.at[s])

    def wait_slot(s):
        _row_copy_loop(tm, lambda t, k: row_copy(s, 0, 0).wait())

    @pl.when(i >= 2)
    def _():
        wait_slot(slot)

    _to_row_tiles(src_ref.at[slot], h1_ref[...])
    _row_copy_loop(tm, lambda t, k: row_copy(slot, t, k).start())

    @pl.when(i == n_steps - 1)
    def _():
        if n_steps >= 2:
            wait_slot(1 - slot)
        wait_slot(slot)


def _dispatch(h1, code, offs, zrow, zflag, n_rows_sorted):
    n = h1.shape[0]
    tm = min(ROW_TM, n)
    return pl.pallas_call(
        functools.partial(_dispatch_kernel, n_steps=n // tm),
        grid_spec=pltpu.PrefetchScalarGridSpec(
            num_scalar_prefetch=3,
            grid=(n // tm,),
            in_specs=[
                pl.BlockSpec((tm * TOP_K,), lambda i, zr, zf, of: (i,), memory_space=pltpu.SMEM),
                pl.BlockSpec((tm, D_MODEL), lambda i, zr, zf, of: (i, 0)),
            ],
            out_specs=pl.BlockSpec(memory_space=pl.ANY),
            scratch_shapes=[pltpu.VMEM((2, tm * ROW_SUB, LANES), f32), pltpu.VMEM((MOE_T * ROW_SUB, LANES), f32),
                            pltpu.SemaphoreType.DMA((2,)), pltpu.SemaphoreType.DMA],
        ),
        out_shape=jax.ShapeDtypeStruct((n_rows_sorted * ROW_SUB, LANES), f32),
        compiler_params=pltpu.CompilerParams(
            dimension_semantics=("arbitrary",), vmem_limit_bytes=VMEM_LIMIT_BYTES),
        name="moe_dispatch",
    )(zrow, zflag, offs, code, h1)


def _ffn_kernel(te_ref, nt_ref, seg_ref, nxt_ref, x_ref, wgu_hbm, wd_hbm, bgu_ref, bd_ref, perm_ref, y_ref,
                wgu_f, wd_f, wgu_s, wd_s, sems):
    j = pl.program_id(0)

    def weight_copies(e, s):
        return (pltpu.make_async_copy(wgu_hbm.at[0, e], wgu_f.at[s], sems.at[s, 0]),
                pltpu.make_async_copy(wd_hbm.at[0, e], wd_f.at[s], sems.at[s, 1]))

    @pl.when(j < nt_ref[0])
    def _():
        slot = seg_ref[j] % 2

        @pl.when(j == 0)
        def _():
            for cp in weight_copies(te_ref[0], 0):
                cp.start()

        @pl.when(jnp.logical_or(j == 0, te_ref[j] != te_ref[jnp.maximum(j - 1, 0)]))
        def _():
            for cp in weight_copies(te_ref[j], slot):
                cp.wait()

            @pl.when(nxt_ref[j] >= 0)
            def _():
                for cp in weight_copies(nxt_ref[j], 1 - slot):
                    cp.start()

            for r0 in range(0, D_MODEL, LANES):
                wgu_s[r0:r0 + LANES, :] = wgu_f[slot, r0:r0 + LANES, :].astype(bf16)
            for r0 in range(0, D_FF, LANES):
                wd_s[r0:r0 + LANES, :] = _dot(perm_ref[...], wd_f[slot, r0:r0 + LANES, :].astype(bf16)).astype(bf16)

        x = _from_row_tiles(x_ref, MOE_T).astype(bf16)
        acc = jnp.broadcast_to(bd_ref[0], (MOE_T, D_MODEL))
        even = (lax.broadcasted_iota(i32, (MOE_T, LANES), 1) & 1) == 0
        for c0 in range(0, D_FF, FFN_FC):
            cols = slice(2 * c0, 2 * (c0 + FFN_FC))
            hgu = _dot(x, wgu_s[:, cols]) + bgu_ref[0, :, cols]
            zs = []
            for j0 in range(0, 2 * FFN_FC, 2 * LANES):
                lo = hgu[:, j0:j0 + LANES]
                hi = hgu[:, j0 + LANES:j0 + 2 * LANES]
                gate = jnp.where(even, lo, pltpu.roll(hi, 1, axis=1))
                up = jnp.where(even, pltpu.roll(lo, LANES - 1, axis=1), hi)
                gate = jnp.minimum(gate, SWIGLU_LIMIT)
                up = jnp.clip(up, -SWIGLU_LIMIT, SWIGLU_LIMIT)
                glu = gate * _sigmoid(gate * SWIGLU_ALPHA)
                zs.append(((up + 1.0) * glu).astype(bf16))
            acc = acc + _dot(jnp.concatenate(zs, axis=1), wd_s[c0:c0 + FFN_FC, :])
        _to_row_tiles(y_ref, acc)

    @pl.when(j >= nt_ref[0])
    def _():
        y_ref[...] = jnp.zeros_like(y_ref)


def _ffn(xs, tile_e, n_tiles, seg, nxt, wgu, wd, bgu, bd):
    p = xs.shape[0] // ROW_SUB
    last = lambda j, te, nt, sg, nx: jnp.minimum(j, nt[0] - 1)
    bmap = lambda j, te, nt, sg, nx: (te[j], 0, 0)
    r = np.arange(LANES)
    perm = np.zeros((LANES, LANES), np.float32)
    perm[r, (r % 2) * (LANES // 2) + r // 2] = 1.0
    return pl.pallas_call(
        _ffn_kernel,
        grid_spec=pltpu.PrefetchScalarGridSpec(
            num_scalar_prefetch=4,
            grid=(p // MOE_T,),
            in_specs=[
                pl.BlockSpec((MOE_T * ROW_SUB, LANES), lambda j, te, nt, sg, nx: (last(j, te, nt, sg, nx), 0)),
                pl.BlockSpec(memory_space=pl.ANY),
                pl.BlockSpec(memory_space=pl.ANY),
                pl.BlockSpec((1, 1, 2 * D_FF), bmap),
                pl.BlockSpec((1, 1, D_MODEL), bmap),
                pl.BlockSpec((LANES, LANES), lambda j, te, nt, sg, nx: (0, 0)),
            ],
            out_specs=pl.BlockSpec((MOE_T * ROW_SUB, LANES), lambda j, te, nt, sg, nx: (j, 0)),
            scratch_shapes=[pltpu.VMEM((2, D_MODEL, 2 * D_FF), f32), pltpu.VMEM((2, D_FF, D_MODEL), f32),
                            pltpu.VMEM((D_MODEL, 2 * D_FF), bf16), pltpu.VMEM((D_FF, D_MODEL), bf16),
                            pltpu.SemaphoreType.DMA((2, 2))],
        ),
        out_shape=jax.ShapeDtypeStruct((p * ROW_SUB, LANES), f32),
        compiler_params=pltpu.CompilerParams(
            dimension_semantics=("arbitrary",), vmem_limit_bytes=VMEM_LIMIT_BYTES),
        name="moe_ffn",
    )(tile_e, n_tiles, seg, nxt, xs, wgu, wd, bgu, bd, jnp.asarray(perm, dtype=bf16))


def _combine_kernel(offs_ref, code_ref, nxt_ref, h1_ref, tw_ref, g2_ref, b2_ref, ys_ref, o_ref, ybuf_ref, sems, *,
                    n_steps):
    tm = h1_ref.shape[0]
    i = pl.program_id(0)
    slot = i % 2

    def row_copy(c_ref, s, t, k):
        return pltpu.make_async_copy(ys_ref.at[_tile_rows(_sorted_row(offs_ref, c_ref[t * TOP_K + k]))],
                                     ybuf_ref.at[s, k, _tile_rows(t)], sems.at[s])

    @pl.when(i == 0)
    def _():
        _row_copy_loop(tm, lambda t, k: row_copy(code_ref, 0, t, k).start())

    @pl.when(i + 1 < n_steps)
    def _():
        _row_copy_loop(tm, lambda t, k: row_copy(nxt_ref, 1 - slot, t, k).start())

    _row_copy_loop(tm, lambda t, k: row_copy(code_ref, slot, 0, 0).wait())

    tw = tw_ref[...]
    ffn = tw[:, 0:1] * _from_row_tiles(ybuf_ref.at[slot, 0], tm)
    for k in range(1, TOP_K):
        ffn = ffn + tw[:, k:k + 1] * _from_row_tiles(ybuf_ref.at[slot, k], tm)
    o_ref[...] = _layer_norm(DN_ALPHA * h1_ref[...] + ffn, g2_ref[...], b2_ref[...])


def _combine(h1, tw, code, offs, ys, g2, b2):
    n = h1.shape[0]
    tm = min(ROW_TM, n)
    n_steps = n // tm
    return pl.pallas_call(
        functools.partial(_combine_kernel, n_steps=n_steps),
        grid_spec=pltpu.PrefetchScalarGridSpec(
            num_scalar_prefetch=1,
            grid=(n_steps,),
            in_specs=[
                pl.BlockSpec((tm * TOP_K,), lambda i, of: (i,), memory_space=pltpu.SMEM),
                pl.BlockSpec((tm * TOP_K,), lambda i, of: (jnp.minimum(i + 1, n_steps - 1),),
                             memory_space=pltpu.SMEM),
                pl.BlockSpec((tm, D_MODEL), lambda i, of: (i, 0)),
                pl.BlockSpec((tm, TOP_K), lambda i, of: (i, 0)),
                pl.BlockSpec(g2.shape, lambda i, of: (0, 0)), pl.BlockSpec(b2.shape, lambda i, of: (0, 0)),
                pl.BlockSpec(memory_space=pl.ANY),
            ],
            out_specs=pl.BlockSpec((tm, D_MODEL), lambda i, of: (i, 0)),
            scratch_shapes=[pltpu.VMEM((2, TOP_K, tm * ROW_SUB, LANES), f32), pltpu.SemaphoreType.DMA((2,))],
        ),
        out_shape=jax.ShapeDtypeStruct((n, D_MODEL), f32),
        compiler_params=pltpu.CompilerParams(
            dimension_semantics=("arbitrary",), vmem_limit_bytes=VMEM_LIMIT_BYTES),
        name="moe_combine_ln",
    )(offs, code, code, h1, tw, g2, b2, ys)


def _routing_tables(counts, n_tiles_max):
    padded = ((counts + MOE_T - 1) // MOE_T) * MOE_T
    ends = jnp.cumsum(padded)
    offs = ends - padded
    n_tiles = (ends[-1] // MOE_T).astype(i32)
    tile_start = jnp.minimum(jnp.arange(n_tiles_max, dtype=i32), n_tiles - 1) * MOE_T
    tile_e = jnp.minimum(jnp.sum((ends[None, :] <= tile_start[:, None]).astype(i32), axis=1), N_EXPERTS - 1)
    trailing = jnp.arange(n_tiles_max - N_EXPERTS, n_tiles_max, dtype=i32)
    zrow = jnp.concatenate([(ends - MOE_T).astype(i32), trailing * MOE_T])
    zflag = jnp.concatenate([(padded > counts).astype(i32), (trailing >= n_tiles).astype(i32)])
    seg = jnp.cumsum(jnp.concatenate([jnp.zeros((1,), i32), (tile_e[1:] != tile_e[:-1]).astype(i32)]))
    e_ids = jnp.arange(N_EXPERTS, dtype=i32)
    later = (e_ids[None, :] > e_ids[:, None]) & (counts[None, :] > 0)
    next_e = jnp.min(jnp.where(later, e_ids[None, :], N_EXPERTS), axis=1)
    nxt = jnp.where(next_e < N_EXPERTS, next_e, -1)[tile_e]
    return offs.astype(i32), tile_e, n_tiles.reshape(1), zrow, zflag, seg.astype(i32), nxt.astype(i32)


def kernel(x, w_in, hg_lb_logits, hg_norm_g, da_lambda, da_norm_g, w_branch_a, w_branch_b, w_out, ln1_g, ln1_b,
           router_w, router_b, w_gate_up, b_gate_up, w_down, b_down, ln2_g, ln2_b):
    batch, seq, d = x.shape
    assert d == D_MODEL and w_in.shape[0] == DEPTH == 1
    n = batch * seq
    x2d = x.reshape(n, d)

    w = w_in[0]
    o_hi, o_ga = 2 * HG_KEY, 2 * HG_KEY + 2 * HG_WIDTH + 2 * DA_QK + DA_WIDTH
    w_all = jnp.concatenate([w[:, 0:o_hi], w[:, o_ga:], w[:, o_hi:o_ga]], axis=1).astype(bf16)
    pf, pb = _inproj(x2d, w_all, o_hi)

    o_a = _hgrn(pf, pb, hg_lb_logits, hg_norm_g[0].reshape(1, HG_WIDTH), batch, seq)
    o_b = _attn(pb, da_lambda[0], da_norm_g[0], batch, seq)

    h1, tw, code, cnt = _merge(
        o_a, o_b, pb, x2d, w_branch_a[0].astype(bf16), w_branch_b[0].astype(bf16), w_out[0].astype(bf16),
        ln1_g[0].reshape(1, d), ln1_b[0].reshape(1, d), router_w[0].astype(bf16), router_b[0].reshape(1, N_EXPERTS))

    n_rows_sorted = n * TOP_K + N_EXPERTS * MOE_T
    offs, tile_e, n_tiles, zrow, zflag, seg, nxt = _routing_tables(cnt[0].astype(i32), n_rows_sorted // MOE_T)
    code = code.reshape(-1)

    xs = _dispatch(h1, code, offs, zrow, zflag, n_rows_sorted)
    ys = _ffn(xs, tile_e, n_tiles, seg, nxt, w_gate_up, w_down, b_gate_up[0][:, None, :], b_down[0][:, None, :])
    out = _combine(h1, tw, code, offs, ys, ln2_g[0].reshape(1, d), ln2_b[0].reshape(1, d))
    return out.reshape(batch, seq, d)
```

```python
import functools
import math

import jax
import jax.numpy as jnp
import numpy as np
from jax import lax
from jax.experimental import pallas as pl
from jax.experimental.pallas import tpu as pltpu

f32 = jnp.float32
bf16 = jnp.bfloat16
i32 = jnp.int32

D_MODEL = 1024
DEPTH = 1
HG_HEADS = 4
HG_DK = 128
HG_DV = 128
HG_KEY = HG_HEADS * HG_DK
HG_WIDTH = HG_HEADS * HG_DV
DA_HEADS = 4
DA_HD = 64
DA_DV = 2 * DA_HD
DA_QK = DA_HEADS * 2 * DA_HD
DA_WIDTH = DA_HEADS * DA_DV
N_EXPERTS = 32
TOP_K = 4
D_FF = 1024
SWIGLU_LIMIT = 7.0
SWIGLU_ALPHA = 1.702
DN_ALPHA = (2.0 * DEPTH) ** 0.25
LN_EPS = 1e-5
NORM_EPS = 1e-6
LAMBDA_INIT = 0.8 - 0.6 * math.exp(-0.3 * 0)
LOG2_E = 1.0 / math.log(2.0)

VMEM_LIMIT_BYTES = 52 * 1024 * 1024

PROJ_TM = 2048
PROJ_TN = 512
HG_CHUNK = 256
HG_ROWS = 512
ATT_T = 256
ATT_R = 128
ATT_U = 4
MERGE_TM = 1024
MOE_T = 256
ROW_TM = 256
FFN_FC = 1024
LANES = 128
ROW_SUB = D_MODEL // LANES

PB_GA = 0
PB_GB = PB_GA + D_MODEL
PB_HI = PB_GB + D_MODEL
PB_HG = PB_HI + HG_WIDTH
PB_DQ = PB_HG + HG_WIDTH
PB_DK = PB_DQ + DA_QK
PB_DV = PB_DK + DA_QK


def _sigmoid(x):
    return 0.5 * jnp.tanh(0.5 * x) + 0.5


def _dot(a, b):
    return jnp.dot(a, b, preferred_element_type=f32)


def _dot_nt(a, b):
    return lax.dot_general(a, b, (((1,), (1,)), ((), ())), preferred_element_type=f32)


def _dot_tn(a, b):
    return lax.dot_general(a, b, (((0,), (0,)), ((), ())), preferred_element_type=f32)


def _inproj_kernel(x_ref, w_ref, of_ref, ob_ref, xb_ref, *, n_f32_tiles):
    j = pl.program_id(1)

    @pl.when(j == 0)
    def _():
        xb_ref[...] = x_ref[...].astype(bf16)

    r = _dot(xb_ref[...], w_ref[...])

    @pl.when(j < n_f32_tiles)
    def _():
        of_ref[...] = r

    @pl.when(j >= n_f32_tiles)
    def _():
        ob_ref[...] = r.astype(bf16)


def _inproj(x2d, w, n_f32_cols):
    n, k = x2d.shape
    m = w.shape[1]
    tm = min(PROJ_TM, n)
    nf = n_f32_cols // PROJ_TN
    return pl.pallas_call(
        functools.partial(_inproj_kernel, n_f32_tiles=nf),
        grid=(n // tm, m // PROJ_TN),
        in_specs=[pl.BlockSpec((tm, k), lambda i, j: (i, 0)),
                  pl.BlockSpec((k, PROJ_TN), lambda i, j: (0, j))],
        out_specs=[pl.BlockSpec((tm, PROJ_TN), lambda i, j: (i, jnp.minimum(j, nf - 1))),
                   pl.BlockSpec((tm, PROJ_TN), lambda i, j: (i, jnp.maximum(j - nf, 0)))],
        out_shape=[jax.ShapeDtypeStruct((n, n_f32_cols), f32), jax.ShapeDtypeStruct((n, m - n_f32_cols), bf16)],
        scratch_shapes=[pltpu.VMEM((tm, k), bf16)],
        compiler_params=pltpu.CompilerParams(
            dimension_semantics=("parallel", "arbitrary"), vmem_limit_bytes=VMEM_LIMIT_BYTES),
        name="inproj",
    )(x2d, w)


def _shift_down(x, d):
    n = x.shape[0]
    if d % 8 == 0:
        return jnp.concatenate([x[n - d:], x[:n - d]], axis=0)
    return pltpu.roll(x, d, axis=0)


def _shift_up(x, d):
    n = x.shape[0]
    if d % 8 == 0:
        return jnp.concatenate([x[d:], x[:d]], axis=0)
    return pltpu.roll(x, n - d, axis=0)


def _hgrn_chunk_head(qv, fl, v, g, lb, ng, state, tril, lvl, row, eye):
    c = qv.shape[0]
    f = lb + (1.0 - lb) * _sigmoid(fl)
    logf = jnp.log(f)
    kk = 1.0 - f
    qa = qv * _sigmoid(qv)

    l1 = logf.astype(bf16)
    r1 = logf - l1.astype(f32)
    l2 = r1.astype(bf16)
    l3 = (r1 - l2.astype(f32)).astype(bf16)
    bc = _dot(tril, jnp.concatenate([l1, l2, l3], axis=1))
    b = bc[:, 0:HG_DK] + bc[:, HG_DK:2 * HG_DK] + bc[:, 2 * HG_DK:3 * HG_DK]

    blk = lvl.shape[0]
    nblk = c // blk
    assert nblk in (1, 2)
    qa_b, kk_b = qa.astype(bf16), kk.astype(bf16)
    diag = [jnp.where(lvl == -1, _dot_nt(qa_b[i * blk:(i + 1) * blk], kk_b[i * blk:(i + 1) * blk]), 0.0)
            for i in range(nblk)]
    off_diag = None
    filled = b
    d = 1
    level = 0
    while d < c:
        upper = (row & d) != 0
        ref_b = jnp.where(upper, _shift_down(filled, d), filled)
        e = jnp.exp2(jnp.abs(b - ref_b) * (-LOG2_E))
        mixed = (jnp.where(upper, qa, kk) * e).astype(bf16)
        if d < blk:
            for i in range(nblk):
                m_i = mixed[i * blk:(i + 1) * blk]
                diag[i] = jnp.where(lvl == level, _dot_nt(m_i, m_i), diag[i])
        else:
            off_diag = _dot_nt(mixed[d:2 * d], mixed[0:d])
        filled = jnp.where(upper, filled, _shift_up(filled, d))
        d *= 2
        level += 1
    if nblk == 1:
        attn = diag[0]
    else:
        attn = jnp.concatenate([jnp.concatenate([diag[0], jnp.zeros_like(diag[0])], axis=1),
                                jnp.concatenate([off_diag, diag[1]], axis=1)], axis=0)

    o_intra = _dot(attn.astype(bf16), v)
    o_inter = _dot((qa * jnp.exp(b)).astype(bf16), state.astype(bf16))

    b_last = b[c - 1:c, :]
    k_dec = (kk * jnp.exp(b_last - b)).astype(bf16)
    dcol = jnp.sum(jnp.where(eye, jnp.broadcast_to(jnp.exp(b_last), eye.shape), 0.0), axis=1, keepdims=True)
    new_state = dcol * state + _dot_tn(k_dec, v)

    o = o_inter + o_intra
    ms = jnp.mean(o * o, axis=1, keepdims=True)
    o = o * lax.rsqrt(ms + NORM_EPS) * ng
    o = o * _sigmoid(g.astype(f32))
    return o.astype(bf16), new_state


def _hgrn_kernel(q_ref, f_ref, i_ref, g_ref, lbl_ref, ng_ref, tril_ref, lvl_ref, o_ref, st_ref, *, chunk, n_chunks):
    @pl.when(pl.program_id(1) == 0)
    def _():
        st_ref[...] = jnp.zeros_like(st_ref)

    lbl = lbl_ref[...]
    ex = jnp.exp(lbl - jnp.max(lbl, axis=0, keepdims=True))
    lb_all = ex[0:1, :] / jnp.sum(ex, axis=0, keepdims=True)
    ng_all = ng_ref[...]
    tril = tril_ref[...]
    lvl = lvl_ref[...]
    row = lax.broadcasted_iota(i32, (chunk, HG_DK), 0)
    eye = lax.broadcasted_iota(i32, (HG_DK, HG_DV), 0) == lax.broadcasted_iota(i32, (HG_DK, HG_DV), 1)

    def body(ci, carry):
        r0 = pl.multiple_of(ci * chunk, chunk)
        for h in range(HG_HEADS):
            cs = pl.ds(h * HG_DK, HG_DK)
            out, new_state = _hgrn_chunk_head(
                q_ref[pl.ds(r0, chunk), cs], f_ref[pl.ds(r0, chunk), cs],
                i_ref[pl.ds(r0, chunk), cs], g_ref[pl.ds(r0, chunk), cs],
                lb_all[:, h * HG_DK:(h + 1) * HG_DK], ng_all[:, h * HG_DV:(h + 1) * HG_DV],
                st_ref[h], tril, lvl, row, eye)
            o_ref[pl.ds(r0, chunk), cs] = out
            st_ref[h] = new_state
        return carry

    lax.fori_loop(0, n_chunks, body, 0)


def _hgrn_consts(chunk):
    t = np.arange(chunk)
    tril = (t[None, :] <= t[:, None]).astype(np.float32)
    x = t[:, None] ^ t[None, :]
    lvl = np.where(x > 0, np.floor(np.log2(np.maximum(x, 1))).astype(np.int32), -1)
    lvl = np.where(t[:, None] >= t[None, :], lvl, -2).astype(np.int32)
    return jnp.asarray(tril, dtype=bf16), jnp.asarray(lvl, dtype=i32)


def _hgrn(pf, pb, lb_logits, norm_g, batch, seq):
    n = batch * seq
    rows = min(HG_ROWS, seq)
    chunk = min(HG_CHUNK, rows)
    spb = seq // rows
    tril, lvl = _hgrn_consts(chunk)
    lvl = lvl[:min(chunk, LANES), :min(chunk, LANES)]
    row_blk = lambda b, s: b * spb + s
    return pl.pallas_call(
        functools.partial(_hgrn_kernel, chunk=chunk, n_chunks=rows // chunk),
        grid=(batch, spb),
        in_specs=[
            pl.BlockSpec((rows, HG_KEY), lambda b, s: (row_blk(b, s), 0)),
            pl.BlockSpec((rows, HG_KEY), lambda b, s: (row_blk(b, s), 1)),
            pl.BlockSpec((rows, HG_WIDTH), lambda b, s: (row_blk(b, s), PB_HI // HG_WIDTH)),
            pl.BlockSpec((rows, HG_WIDTH), lambda b, s: (row_blk(b, s), PB_HG // HG_WIDTH)),
            pl.BlockSpec(lb_logits.shape, lambda b, s: (0, 0)),
            pl.BlockSpec((1, HG_WIDTH), lambda b, s: (0, 0)),
            pl.BlockSpec((chunk, chunk), lambda b, s: (0, 0)),
            pl.BlockSpec(lvl.shape, lambda b, s: (0, 0)),
        ],
        out_specs=pl.BlockSpec((rows, HG_WIDTH), lambda b, s: (row_blk(b, s), 0)),
        out_shape=jax.ShapeDtypeStruct((n, HG_WIDTH), bf16),
        scratch_shapes=[pltpu.VMEM((HG_HEADS, HG_DK, HG_DV), f32)],
        compiler_params=pltpu.CompilerParams(
            dimension_semantics=("parallel", "arbitrary"), vmem_limit_bytes=VMEM_LIMIT_BYTES),
        name="hgrn2",
    )(pf, pf, pb, pb, lb_logits, norm_g, tril, lvl)


def _attn_kernel(lam_ref, ng_ref, q_ref, k_ref, v_ref, o_ref, qs_ref, m_ref, acc_ref, *, t):
    qi = pl.program_id(1)
    blk = 2 * DA_HD

    lane = lax.broadcasted_iota(i32, (t, blk), 1)
    for h in range(DA_HEADS):
        q = q_ref[:, h * blk:(h + 1) * blk] * jnp.asarray(DA_HD ** -0.5, bf16)
        zero = jnp.zeros_like(q)
        qs_ref[h, 0:t, :] = jnp.where(lane < DA_HD, q, zero)
        qs_ref[h, t:2 * t, :] = jnp.where(lane >= DA_HD, q, zero)
    m_ref[...] = jnp.full_like(m_ref, -jnp.inf)
    acc_ref[...] = jnp.zeros_like(acc_ref)

    col = lax.broadcasted_iota(i32, (1, t), 1)
    ones = jnp.ones((t, DA_DV), bf16)

    def step(kt, masked):
        k0 = pl.multiple_of(kt * t, t)
        rel = ((kt - qi) * t + col).astype(f32)
        if masked:
            rr = lax.broadcasted_iota(i32, (2 * t, t), 0)
            rr = jnp.where(rr >= t, rr - t, rr)
            causal = lax.broadcasted_iota(i32, (2 * t, t), 1) <= rr
        for h in range(DA_HEADS):
            cs = pl.ds(h * blk, blk)
            k = k_ref[pl.ds(k0, t), cs]
            vo = jnp.concatenate([v_ref[pl.ds(k0, t), cs], ones], axis=1)
            bias = (2.0 ** (-8.0 * (h + 1) / DA_HEADS)) * rel
            for r0 in range(0, 2 * t, ATT_R):
                rows = slice(r0, r0 + ATT_R)
                s = _dot_nt(qs_ref[h, rows, :], k) + bias
                if masked:
                    s = jnp.where(causal[rows, :], s, -jnp.inf)
                m_old = m_ref[h, rows, :]
                m_new = jnp.maximum(m_old, jnp.max(s, axis=1, keepdims=True))
                alpha = jnp.exp(m_old - m_new)
                p = jnp.exp(s - jnp.concatenate([m_new] * (t // DA_DV), axis=1)).astype(bf16)
                acc_ref[h, rows, :] = jnp.concatenate([alpha, alpha], axis=1) * acc_ref[h, rows, :] + _dot(p, vo)
                m_ref[h, rows, :] = m_new

    def loop_body(kp, carry):
        for u in range(ATT_U):
            step(ATT_U * kp + u, False)
        return carry

    lax.fori_loop(0, qi // ATT_U, loop_body, 0)

    def tail_body(kt, carry):
        step(kt, False)
        return carry

    lax.fori_loop((qi // ATT_U) * ATT_U, qi, tail_body, 0)

    step(qi, True)

    lp = lam_ref[...]
    lam = (jnp.exp(jnp.sum(lp[0:1, :] * lp[1:2, :], axis=1, keepdims=True))
           - jnp.exp(jnp.sum(lp[2:3, :] * lp[3:4, :], axis=1, keepdims=True)) + LAMBDA_INIT)
    for h in range(DA_HEADS):
        acc = acc_ref[h]
        o_all = acc[:, 0:DA_DV] / acc[:, DA_DV:2 * DA_DV]
        o = o_all[0:t, :] - lam * o_all[t:2 * t, :]
        ms = jnp.mean(o * o, axis=1, keepdims=True)
        o = o * lax.rsqrt(ms + NORM_EPS) * ng_ref[:, h * DA_DV:(h + 1) * DA_DV] * (1.0 - LAMBDA_INIT)
        o_ref[:, h * DA_DV:(h + 1) * DA_DV] = o.astype(bf16)


def _attn(pb, lam_params, norm_g, batch, seq):
    n = batch * seq
    t = min(ATT_T, seq)
    nq = seq // t
    q0, k0, v0 = PB_DQ // DA_QK, PB_DK // DA_QK, PB_DV // DA_WIDTH
    return pl.pallas_call(
        functools.partial(_attn_kernel, t=t),
        grid=(batch, nq),
        in_specs=[
            pl.BlockSpec(lam_params.shape, lambda b, i: (0, 0)),
            pl.BlockSpec((1, DA_WIDTH), lambda b, i: (0, 0)),
            pl.BlockSpec((t, DA_QK), lambda b, i: (b * nq + i, q0)),
            pl.BlockSpec((seq, DA_QK), lambda b, i: (b, k0)),
            pl.BlockSpec((seq, DA_WIDTH), lambda b, i: (b, v0)),
        ],
        out_specs=pl.BlockSpec((t, DA_WIDTH), lambda b, i: (b * nq + i, 0)),
        out_shape=jax.ShapeDtypeStruct((n, DA_WIDTH), bf16),
        scratch_shapes=[pltpu.VMEM((DA_HEADS, 2 * t, 2 * DA_HD), bf16), pltpu.VMEM((DA_HEADS, 2 * t, DA_DV), f32),
                        pltpu.VMEM((DA_HEADS, 2 * t, 2 * DA_DV), f32)],
        compiler_params=pltpu.CompilerParams(
            dimension_semantics=("parallel", "arbitrary"), vmem_limit_bytes=VMEM_LIMIT_BYTES),
        name="diff_attn",
    )(lam_params, norm_g.reshape(1, DA_WIDTH), pb, pb, pb)


def _layer_norm(y, g, b):
    mu = jnp.mean(y, axis=1, keepdims=True)
    yc = y - mu
    var = jnp.mean(yc * yc, axis=1, keepdims=True)
    return yc * lax.rsqrt(var + LN_EPS) * g + b


def _assemble4(cols, dtype):
    tm = cols[0].shape[0]
    lane = lax.broadcasted_iota(i32, (tm, TOP_K), 1)
    out = jnp.broadcast_to(cols[TOP_K - 1], (tm, TOP_K))
    for k in range(TOP_K - 2, -1, -1):
        out = jnp.where(lane == k, jnp.broadcast_to(cols[k], (tm, TOP_K)), out)
    return out.astype(dtype)


def _merge_kernel(oa_ref, ob_ref, ga_ref, gb_ref, x_ref, wa_ref, wb_ref, wo_ref, g1_ref, b1_ref, rw_ref, rb_ref,
                  tri_ref, h1_ref, idx_ref, tw_ref, rank_ref, cnt_ref, carry_ref):
    @pl.when(pl.program_id(0) == 0)
    def _():
        carry_ref[...] = jnp.zeros_like(carry_ref)

    a = _dot(oa_ref[...], wa_ref[...])
    b = _dot(ob_ref[...], wb_ref[...])
    merged = _sigmoid(ga_ref[...].astype(f32)) * a + _sigmoid(gb_ref[...].astype(f32)) * b
    mix = _dot(merged.astype(bf16), wo_ref[...])
    h1 = _layer_norm(DN_ALPHA * x_ref[...] + mix, g1_ref[...], b1_ref[...])
    h1_ref[...] = h1

    logits = _dot(h1.astype(bf16), rw_ref[...]) + rb_ref[...]
    tm = logits.shape[0]
    lane = lax.broadcasted_iota(i32, (tm, N_EXPERTS), 1).astype(f32)
    work = logits
    vals, idxs = [], []
    for _ in range(TOP_K):
        mk = jnp.max(work, axis=1, keepdims=True)
        ik = jnp.min(jnp.where(work == mk, lane, float(N_EXPERTS)), axis=1, keepdims=True)
        vals.append(mk)
        idxs.append(ik)
        work = jnp.where(lane == ik, -jnp.inf, work)
    es = [jnp.exp(v - vals[0]) for v in vals]
    den = es[0] + es[1] + es[2] + es[3]
    tw_ref[...] = _assemble4([e / den for e in es], f32)
    idx_ref[...] = _assemble4(idxs, i32)

    onehot = jnp.zeros((tm, N_EXPERTS), f32)
    for ik in idxs:
        onehot = onehot + jnp.where(lane == ik, 1.0, 0.0)
    before = _dot(tri_ref[...], onehot.astype(bf16)) + carry_ref[...]
    ranks = [jnp.sum(jnp.where(lane == ik, before, 0.0), axis=1, keepdims=True) for ik in idxs]
    rank_ref[...] = _assemble4(ranks, i32)
    total = carry_ref[...] + jnp.sum(onehot, axis=0, keepdims=True)
    carry_ref[...] = total
    cnt_ref[...] = total


def _merge(o_a, o_b, pb, x2d, wa, wb, wo, g1, b1, rw, rb):
    n = x2d.shape[0]
    tm = min(MERGE_TM, n)
    t = np.arange(tm)
    tri = jnp.asarray((t[None, :] < t[:, None]).astype(np.float32), dtype=bf16)
    row = lambda i: (i, 0)
    const = lambda i: (0, 0)
    return pl.pallas_call(
        _merge_kernel,
        grid=(n // tm,),
        in_specs=[
            pl.BlockSpec((tm, HG_WIDTH), row),
            pl.BlockSpec((tm, DA_WIDTH), row),
            pl.BlockSpec((tm, D_MODEL), lambda i: (i, PB_GA // D_MODEL)),
            pl.BlockSpec((tm, D_MODEL), lambda i: (i, PB_GB // D_MODEL)),
            pl.BlockSpec((tm, D_MODEL), row),
            pl.BlockSpec(wa.shape, const), pl.BlockSpec(wb.shape, const), pl.BlockSpec(wo.shape, const),
            pl.BlockSpec(g1.shape, const), pl.BlockSpec(b1.shape, const),
            pl.BlockSpec(rw.shape, const), pl.BlockSpec(rb.shape, const),
            pl.BlockSpec((tm, tm), const),
        ],
        out_specs=[
            pl.BlockSpec((tm, D_MODEL), row),
            pl.BlockSpec((tm, TOP_K), row), pl.BlockSpec((tm, TOP_K), row), pl.BlockSpec((tm, TOP_K), row),
            pl.BlockSpec((1, N_EXPERTS), const),
        ],
        out_shape=[
            jax.ShapeDtypeStruct((n, D_MODEL), f32),
            jax.ShapeDtypeStruct((n, TOP_K), i32), jax.ShapeDtypeStruct((n, TOP_K), f32),
            jax.ShapeDtypeStruct((n, TOP_K), i32),
            jax.ShapeDtypeStruct((1, N_EXPERTS), f32),
        ],
        scratch_shapes=[pltpu.VMEM((1, N_EXPERTS), f32)],
        compiler_params=pltpu.CompilerParams(
            dimension_semantics=("arbitrary",), vmem_limit_bytes=VMEM_LIMIT_BYTES),
        name="merge_ln_router",
    )(o_a, o_b, pb, pb, x2d, wa, wb, wo, g1, b1, rw, rb, tri)


def _row_copy_loop(n_rows, start_one):
    def body(t, carry):
        for k in range(TOP_K):
            start_one(t, k)
        return carry
    lax.fori_loop(0, n_rows, body, 0)


def _to_row_tiles(dst_ref, x):
    rows = x.shape[0]
    for c in range(ROW_SUB):
        dst_ref[pl.ds(c, rows, stride=ROW_SUB), :] = x[:, c * LANES:(c + 1) * LANES]


def _from_row_tiles(src_ref, rows):
    return jnp.concatenate([src_ref[pl.ds(c, rows, stride=ROW_SUB), :] for c in range(ROW_SUB)], axis=1)


def _tile_rows(r, n=1):
    return pl.ds(pl.multiple_of(r * ROW_SUB, ROW_SUB), n * ROW_SUB)


def _dispatch_kernel(zrow_ref, zflag_ref, pos_ref, h1_ref, xs_ref, src_ref, zeros_ref, sems, zsem, *, n_steps):
    tm = h1_ref.shape[0]
    i = pl.program_id(0)
    slot = i % 2

    @pl.when(i == 0)
    def _():
        zeros_ref[...] = jnp.zeros_like(zeros_ref)
        for e in range(2 * N_EXPERTS):
            @pl.when(zflag_ref[e] == 1)
            def _():
                cp = pltpu.make_async_copy(zeros_ref, xs_ref.at[_tile_rows(zrow_ref[e], MOE_T)], zsem)
                cp.start()
                cp.wait()

    def row_copy(s, t, k):
        return pltpu.make_async_copy(src_ref.at[s, _tile_rows(t)], xs_ref.at[_tile_rows(pos_ref[t * TOP_K + k])],
                                     sems.at[s])

    def wait_slot(s):
        _row_copy_loop(tm, lambda t, k: row_copy(s, t, k).wait())

    @pl.when(i >= 2)
    def _():
        wait_slot(slot)

    _to_row_tiles(src_ref.at[slot], h1_ref[...])
    _row_copy_loop(tm, lambda t, k: row_copy(slot, t, k).start())

    @pl.when(i == n_steps - 1)
    def _():
        if n_steps >= 2:
            wait_slot(1 - slot)
        wait_slot(slot)


def _dispatch(h1, pos_flat, zrow, zflag, n_rows_sorted):
    n = h1.shape[0]
    tm = min(ROW_TM, n)
    return pl.pallas_call(
        functools.partial(_dispatch_kernel, n_steps=n // tm),
        grid_spec=pltpu.PrefetchScalarGridSpec(
            num_scalar_prefetch=2,
            grid=(n // tm,),
            in_specs=[
                pl.BlockSpec((tm * TOP_K,), lambda i, zr, zf: (i,), memory_space=pltpu.SMEM),
                pl.BlockSpec((tm, D_MODEL), lambda i, zr, zf: (i, 0)),
            ],
            out_specs=pl.BlockSpec(memory_space=pl.ANY),
            scratch_shapes=[pltpu.VMEM((2, tm * ROW_SUB, LANES), f32), pltpu.VMEM((MOE_T * ROW_SUB, LANES), f32),
                            pltpu.SemaphoreType.DMA((2,)), pltpu.SemaphoreType.DMA],
        ),
        out_shape=jax.ShapeDtypeStruct((n_rows_sorted * ROW_SUB, LANES), f32),
        compiler_params=pltpu.CompilerParams(
            dimension_semantics=("arbitrary",), vmem_limit_bytes=VMEM_LIMIT_BYTES),
        name="moe_dispatch",
    )(zrow, zflag, pos_flat, h1)


def _ffn_kernel(te_ref, nt_ref, seg_ref, nxt_ref, x_ref, wgu_hbm, wd_hbm, bgu_ref, bd_ref, perm_ref, y_ref,
                wgu_f, wd_f, wgu_s, wd_s, sems):
    j = pl.program_id(0)

    def weight_copies(e, s):
        return (pltpu.make_async_copy(wgu_hbm.at[0, e], wgu_f.at[s], sems.at[s, 0]),
                pltpu.make_async_copy(wd_hbm.at[0, e], wd_f.at[s], sems.at[s, 1]))

    @pl.when(j < nt_ref[0])
    def _():
        slot = seg_ref[j] % 2

        @pl.when(j == 0)
        def _():
            for cp in weight_copies(te_ref[0], 0):
                cp.start()

        @pl.when(jnp.logical_or(j == 0, te_ref[j] != te_ref[jnp.maximum(j - 1, 0)]))
        def _():
            for cp in weight_copies(te_ref[j], slot):
                cp.wait()

            @pl.when(nxt_ref[j] >= 0)
            def _():
                for cp in weight_copies(nxt_ref[j], 1 - slot):
                    cp.start()

            for r0 in range(0, D_MODEL, LANES):
                wgu_s[r0:r0 + LANES, :] = wgu_f[slot, r0:r0 + LANES, :].astype(bf16)
            for r0 in range(0, D_FF, LANES):
                wd_s[r0:r0 + LANES, :] = _dot(perm_ref[...], wd_f[slot, r0:r0 + LANES, :].astype(bf16)).astype(bf16)

        x = _from_row_tiles(x_ref, MOE_T).astype(bf16)
        acc = jnp.broadcast_to(bd_ref[0], (MOE_T, D_MODEL))
        even = (lax.broadcasted_iota(i32, (MOE_T, LANES), 1) & 1) == 0
        for c0 in range(0, D_FF, FFN_FC):
            cols = slice(2 * c0, 2 * (c0 + FFN_FC))
            hgu = _dot(x, wgu_s[:, cols]) + bgu_ref[0, :, cols]
            zs = []
            for j0 in range(0, 2 * FFN_FC, 2 * LANES):
                lo = hgu[:, j0:j0 + LANES]
                hi = hgu[:, j0 + LANES:j0 + 2 * LANES]
                gate = jnp.where(even, lo, pltpu.roll(hi, 1, axis=1))
                up = jnp.where(even, pltpu.roll(lo, LANES - 1, axis=1), hi)
                gate = jnp.minimum(gate, SWIGLU_LIMIT)
                up = jnp.clip(up, -SWIGLU_LIMIT, SWIGLU_LIMIT)
                glu = gate * _sigmoid(gate * SWIGLU_ALPHA)
                zs.append(((up + 1.0) * glu).astype(bf16))
            acc = acc + _dot(jnp.concatenate(zs, axis=1), wd_s[c0:c0 + FFN_FC, :])
        _to_row_tiles(y_ref, acc)

    @pl.when(j >= nt_ref[0])
    def _():
        y_ref[...] = jnp.zeros_like(y_ref)


def _ffn(xs, tile_e, n_tiles, seg, nxt, wgu, wd, bgu, bd):
    p = xs.shape[0] // ROW_SUB
    last = lambda j, te, nt, sg, nx: jnp.minimum(j, nt[0] - 1)
    bmap = lambda j, te, nt, sg, nx: (te[j], 0, 0)
    r = np.arange(LANES)
    perm = np.zeros((LANES, LANES), np.float32)
    perm[r, (r % 2) * (LANES // 2) + r // 2] = 1.0
    return pl.pallas_call(
        _ffn_kernel,
        grid_spec=pltpu.PrefetchScalarGridSpec(
            num_scalar_prefetch=4,
            grid=(p // MOE_T,),
            in_specs=[
                pl.BlockSpec((MOE_T * ROW_SUB, LANES), lambda j, te, nt, sg, nx: (last(j, te, nt, sg, nx), 0)),
                pl.BlockSpec(memory_space=pl.ANY),
                pl.BlockSpec(memory_space=pl.ANY),
                pl.BlockSpec((1, 1, 2 * D_FF), bmap),
                pl.BlockSpec((1, 1, D_MODEL), bmap),
                pl.BlockSpec((LANES, LANES), lambda j, te, nt, sg, nx: (0, 0)),
            ],
            out_specs=pl.BlockSpec((MOE_T * ROW_SUB, LANES), lambda j, te, nt, sg, nx: (j, 0)),
            scratch_shapes=[pltpu.VMEM((2, D_MODEL, 2 * D_FF), f32), pltpu.VMEM((2, D_FF, D_MODEL), f32),
                            pltpu.VMEM((D_MODEL, 2 * D_FF), bf16), pltpu.VMEM((D_FF, D_MODEL), bf16),
                            pltpu.SemaphoreType.DMA((2, 2))],
        ),
        out_shape=jax.ShapeDtypeStruct((p * ROW_SUB, LANES), f32),
        compiler_params=pltpu.CompilerParams(
            dimension_semantics=("arbitrary",), vmem_limit_bytes=VMEM_LIMIT_BYTES),
        name="moe_ffn",
    )(tile_e, n_tiles, seg, nxt, xs, wgu, wd, bgu, bd, jnp.asarray(perm, dtype=bf16))


def _combine_kernel(pos_ref, nxt_ref, h1_ref, tw_ref, g2_ref, b2_ref, ys_ref, o_ref, ybuf_ref, sems, *, n_steps):
    tm = h1_ref.shape[0]
    i = pl.program_id(0)
    slot = i % 2

    def row_copy(p_ref, s, t, k):
        return pltpu.make_async_copy(ys_ref.at[_tile_rows(p_ref[t * TOP_K + k])], ybuf_ref.at[s, k, _tile_rows(t)],
                                     sems.at[s])

    @pl.when(i == 0)
    def _():
        _row_copy_loop(tm, lambda t, k: row_copy(pos_ref, 0, t, k).start())

    @pl.when(i + 1 < n_steps)
    def _():
        _row_copy_loop(tm, lambda t, k: row_copy(nxt_ref, 1 - slot, t, k).start())

    _row_copy_loop(tm, lambda t, k: row_copy(pos_ref, slot, t, k).wait())

    tw = tw_ref[...]
    ffn = tw[:, 0:1] * _from_row_tiles(ybuf_ref.at[slot, 0], tm)
    for k in range(1, TOP_K):
        ffn = ffn + tw[:, k:k + 1] * _from_row_tiles(ybuf_ref.at[slot, k], tm)
    o_ref[...] = _layer_norm(DN_ALPHA * h1_ref[...] + ffn, g2_ref[...], b2_ref[...])


def _combine(h1, tw, pos_flat, ys, g2, b2):
    n = h1.shape[0]
    tm = min(ROW_TM, n)
    n_steps = n // tm
    return pl.pallas_call(
        functools.partial(_combine_kernel, n_steps=n_steps),
        grid=(n_steps,),
        in_specs=[
            pl.BlockSpec((tm * TOP_K,), lambda i: (i,), memory_space=pltpu.SMEM),
            pl.BlockSpec((tm * TOP_K,), lambda i: (jnp.minimum(i + 1, n_steps - 1),), memory_space=pltpu.SMEM),
            pl.BlockSpec((tm, D_MODEL), lambda i: (i, 0)),
            pl.BlockSpec((tm, TOP_K), lambda i: (i, 0)),
            pl.BlockSpec(g2.shape, lambda i: (0, 0)), pl.BlockSpec(b2.shape, lambda i: (0, 0)),
            pl.BlockSpec(memory_space=pl.ANY),
        ],
        out_specs=pl.BlockSpec((tm, D_MODEL), lambda i: (i, 0)),
        out_shape=jax.ShapeDtypeStruct((n, D_MODEL), f32),
        scratch_shapes=[pltpu.VMEM((2, TOP_K, tm * ROW_SUB, LANES), f32), pltpu.SemaphoreType.DMA((2,))],
        compiler_params=pltpu.CompilerParams(
            dimension_semantics=("arbitrary",), vmem_limit_bytes=VMEM_LIMIT_BYTES),
        name="moe_combine_ln",
    )(pos_flat, pos_flat, h1, tw, g2, b2, ys)


def _routing_tables(counts, idx, rank, n_tiles_max):
    padded = ((counts + MOE_T - 1) // MOE_T) * MOE_T
    ends = jnp.cumsum(padded)
    offs = ends - padded
    pos = (offs[idx] + rank).reshape(-1)
    n_tiles = (ends[-1] // MOE_T).astype(i32)
    tile_start = jnp.minimum(jnp.arange(n_tiles_max, dtype=i32), n_tiles - 1) * MOE_T
    tile_e = jnp.minimum(jnp.sum((ends[None, :] <= tile_start[:, None]).astype(i32), axis=1), N_EXPERTS - 1)
    trailing = jnp.arange(n_tiles_max - N_EXPERTS, n_tiles_max, dtype=i32)
    zrow = jnp.concatenate([(ends - MOE_T).astype(i32), trailing * MOE_T])
    zflag = jnp.concatenate([(padded > counts).astype(i32), (trailing >= n_tiles).astype(i32)])
    seg = jnp.cumsum(jnp.concatenate([jnp.zeros((1,), i32), (tile_e[1:] != tile_e[:-1]).astype(i32)]))
    e_ids = jnp.arange(N_EXPERTS, dtype=i32)
    later = (e_ids[None, :] > e_ids[:, None]) & (counts[None, :] > 0)
    next_e = jnp.min(jnp.where(later, e_ids[None, :], N_EXPERTS), axis=1)
    nxt = jnp.where(next_e < N_EXPERTS, next_e, -1)[tile_e]
    return pos.astype(i32), tile_e, n_tiles.reshape(1), zrow, zflag, seg.astype(i32), nxt.astype(i32)


def kernel(x, w_in, hg_lb_logits, hg_norm_g, da_lambda, da_norm_g, w_branch_a, w_branch_b, w_out, ln1_g, ln1_b,
           router_w, router_b, w_gate_up, b_gate_up, w_down, b_down, ln2_g, ln2_b):
    batch, seq, d = x.shape
    assert d == D_MODEL and w_in.shape[0] == DEPTH == 1
    n = batch * seq
    x2d = x.reshape(n, d)

    w = w_in[0]
    o_hi, o_ga = 2 * HG_KEY, 2 * HG_KEY + 2 * HG_WIDTH + 2 * DA_QK + DA_WIDTH
    w_all = jnp.concatenate([w[:, 0:o_hi], w[:, o_ga:], w[:, o_hi:o_ga]], axis=1).astype(bf16)
    pf, pb = _inproj(x2d, w_all, o_hi)

    o_a = _hgrn(pf, pb, hg_lb_logits, hg_norm_g[0].reshape(1, HG_WIDTH), batch, seq)
    o_b = _attn(pb, da_lambda[0], da_norm_g[0], batch, seq)

    h1, idx, tw, rank, cnt = _merge(
        o_a, o_b, pb, x2d, w_branch_a[0].astype(bf16), w_branch_b[0].astype(bf16), w_out[0].astype(bf16),
        ln1_g[0].reshape(1, d), ln1_b[0].reshape(1, d), router_w[0].astype(bf16), router_b[0].reshape(1, N_EXPERTS))

    n_rows_sorted = n * TOP_K + N_EXPERTS * MOE_T
    pos, tile_e, n_tiles, zrow, zflag, seg, nxt = _routing_tables(
        cnt[0].astype(i32), idx, rank, n_rows_sorted // MOE_T)

    xs = _dispatch(h1, pos, zrow, zflag, n_rows_sorted)
    ys = _ffn(xs, tile_e, n_tiles, seg, nxt, w_gate_up, w_down, b_gate_up[0][:, None, :], b_down[0][:, None, :])
    out = _combine(h1, tw, pos, ys, ln2_g[0].reshape(1, d), ln2_b[0].reshape(1, d))
    return out.reshape(batch, seq, d)
```
